```python
import jax, jax.numpy as jnp
from jax import lax
import numpy as np

D_MODEL = 1024
BATCH = 16
SEQ = 4096
DEPTH = 2

MIX_W = 1024
CONV_K = 3
CONV_GROUPS = 8
N_HEADS = 16
HEAD_DIM = 64
N_KV = 4
HPG = N_HEADS // N_KV
KV_W = N_KV * HEAD_DIM
L_CMP = 32
STRIDE_CMP = 16
CMP_HIDDEN = 128
L_SEL = 64
N_SEL_BLOCKS = 16
WINDOW = 512
Q_BLOCK = 16
CHUNK = 128
GM_GROUPS = 8
GM_GW = MIX_W // GM_GROUPS
N_BRANCH = 3
EPS = 1e-6
NEG = -1e30

A_OFF = 0
A_COLS = 4 * MIX_W
B_OFF = A_OFF + A_COLS
B_COLS = 2 * MIX_W + 6 * KV_W + 3 * N_HEADS
C_OFF = B_OFF + B_COLS
C_COLS = 3 * MIX_W
G_OFF = C_OFF + C_COLS
G_COLS = N_BRANCH * D_MODEL
IN_COLS = G_OFF + G_COLS

kernel_name = "hybrid_conv_nsa_gmlp_gated_block"


def rmsnorm(x, g):
    x32 = x.astype(jnp.float32)
    y = x32 * lax.rsqrt(jnp.mean(x32 * x32, axis=-1, keepdims=True) + EPS)
    return y.astype(x.dtype) * g


def layernorm(x, g, b):
    x32 = x.astype(jnp.float32)
    mu = jnp.mean(x32, axis=-1, keepdims=True)
    xc = x32 - mu
    y = xc * lax.rsqrt(jnp.mean(xc * xc, axis=-1, keepdims=True) + EPS)
    return y.astype(x.dtype) * g + b


def masked_softmax(s, mask):
    p = jax.nn.softmax(jnp.where(mask, s, NEG), axis=-1)
    return jnp.where(mask, p, 0.0)


def alibi_slopes():
    i = jnp.arange(1, N_HEADS + 1, dtype=jnp.float32)
    return (2.0 ** (-8.0 * i / N_HEADS)).reshape(N_KV, HPG)


def short_conv_mixer(h, w_in_a, conv_w, conv_b):
    S = h.shape[1]
    b, cg, xin, z = jnp.split(h @ w_in_a, 4, axis=-1)
    y = cg * xin
    yp = jnp.pad(y, ((0, 0), (CONV_K - 1, 0), (0, 0)))
    conv = conv_b + sum(conv_w[k] * yp[:, k:k + S] for k in range(CONV_K))
    return b * conv * jax.nn.silu(z)


def nsa_mixer(h, w_in_b, pos_ck, w_ck1, w_ck2, pos_cv, w_cv1, w_cv2):
    Bsz, S, _ = h.shape
    sizes = [MIX_W] + [KV_W] * 6 + [MIX_W]
    q, kc, vc, ks, vs, kw, vw, z, gl = jnp.split(h @ w_in_b, np.cumsum(sizes).tolist(), axis=-1)

    def heads_kv(t):
        return t.reshape(Bsz, S, N_KV, HEAD_DIM).transpose(0, 2, 1, 3)

    n_cmp = (S - L_CMP) // STRIDE_CMP + 1
    cmp_start = jnp.arange(n_cmp) * STRIDE_CMP
    cmp_end = cmp_start + (L_CMP - 1)
    cmp_idx = cmp_start[:, None] + jnp.arange(L_CMP)[None, :]

    def compress(t, pos, w1, w2):
        blocks = heads_kv(t)[:, :, cmp_idx] + pos
        flat = blocks.reshape(Bsz, N_KV, n_cmp, L_CMP * HEAD_DIM)
        return jax.nn.silu(flat @ w1) @ w2

    k_cmp = compress(kc, pos_ck, w_ck1, w_ck2)
    v_cmp = compress(vc, pos_cv, w_cv1, w_cv2)

    n_sel = S // L_SEL
    k_top = min(N_SEL_BLOCKS, n_sel)
    k_blk = heads_kv(ks).reshape(Bsz, N_KV, n_sel, L_SEL, HEAD_DIM)
    v_blk = heads_kv(vs).reshape(Bsz, N_KV, n_sel, L_SEL, HEAD_DIM)
    sel_ids = jnp.arange(n_sel)
    overlap = ((cmp_start[:, None] <= (sel_ids[None, :] + 1) * L_SEL - 1)
               & (cmp_end[:, None] >= sel_ids[None, :] * L_SEL)).astype(jnp.float32)

    pad = ((0, 0), (0, 0), (WINDOW - 1, 0), (0, 0))
    k_win = jnp.pad(heads_kv(kw), pad)
    v_win = jnp.pad(heads_kv(vw), pad)
    span = Q_BLOCK + WINDOW - 1

    nq = S // Q_BLOCK
    qh = q.reshape(Bsz, nq, Q_BLOCK, N_KV, HPG, HEAD_DIM).transpose(1, 0, 3, 4, 2, 5)
    gh = gl.reshape(Bsz, nq, Q_BLOCK, N_KV, HPG, 3).transpose(1, 0, 3, 4, 2, 5)
    slopes = alibi_slopes()
    scale = HEAD_DIM ** -0.5
    bi = jnp.arange(Bsz)[:, None, None, None]
    gi = jnp.arange(N_KV)[None, :, None, None]

    def block_fn(args):
        ci, qb, gb = args
        t0 = ci * Q_BLOCK
        tq = t0 + jnp.arange(Q_BLOCK)
        d_c = tq[:, None] - cmp_end[None, :]
        s = jnp.einsum('bgnqd,bgkd->bgnqk', qb, k_cmp).astype(jnp.float32) * scale \
            - slopes[:, :, None, None] * d_c.astype(jnp.float32)
        p_cmp = masked_softmax(s, d_c >= 0)
        o_cmp = jnp.einsum('bgnqk,bgkd->bgnqd', p_cmp.astype(v_cmp.dtype), v_cmp)
        imp = jnp.einsum('bgnqk,ks->bgqs', p_cmp, overlap)
        cur = tq // L_SEL
        valid = sel_ids[None, :] * L_SEL <= tq[:, None]
        forced = (sel_ids[None, :] == 0) | (sel_ids[None, :] == cur[:, None]) | (sel_ids[None, :] == cur[:, None] - 1)
        score = jnp.where(forced, jnp.inf, jnp.where(valid, imp, -jnp.inf))
        _, sel = lax.top_k(score, k_top)
        kg = k_blk[bi, gi, sel]
        vg = v_blk[bi, gi, sel]
        spos = sel[..., None] * L_SEL + jnp.arange(L_SEL)
        d_s = (tq[None, None, :, None, None] - spos)[:, :, None]
        s = jnp.einsum('bgnqd,bgqkld->bgnqkl', qb, kg).astype(jnp.float32) * scale \
            - slopes[:, :, None, None, None] * d_s.astype(jnp.float32)
        m = jnp.broadcast_to(d_s >= 0, s.shape)
        flat_shape = (Bsz, N_KV, HPG, Q_BLOCK, k_top * L_SEL)
        p = masked_softmax(s.reshape(flat_shape), m.reshape(flat_shape))
        o_slc = jnp.einsum('bgnqm,bgqmd->bgnqd', p.astype(vg.dtype),
                           vg.reshape(Bsz, N_KV, Q_BLOCK, k_top * L_SEL, HEAD_DIM))
        kwb = lax.dynamic_slice_in_dim(k_win, t0, span, axis=2)
        vwb = lax.dynamic_slice_in_dim(v_win, t0, span, axis=2)
        kpos = t0 - (WINDOW - 1) + jnp.arange(span)
        d_w = tq[:, None] - kpos[None, :]
        m_w = (d_w >= 0) & (d_w < WINDOW) & (kpos[None, :] >= 0)
        s = jnp.einsum('bgnqd,bgkd->bgnqk', qb, kwb).astype(jnp.float32) * scale \
            - slopes[:, :, None, None] * d_w.astype(jnp.float32)
        p = masked_softmax(s, m_w)
        o_win = jnp.einsum('bgnqk,bgkd->bgnqd', p.astype(vwb.dtype), vwb)
        g = jax.nn.sigmoid(gb)
        return g[..., 0:1] * o_cmp + g[..., 1:2] * o_slc + g[..., 2:3] * o_win

    o = lax.map(block_fn, (jnp.arange(nq), qh, gh))
    o = o.transpose(1, 0, 4, 2, 3, 5).reshape(Bsz, S, MIX_W)
    return o * jax.nn.silu(z)


def gmlp_mixer(h, w_in_c, ln_g, ln_b, w_s, b_s):
    Bsz, S, _ = h.shape
    u, v, z = jnp.split(h @ w_in_c, 3, axis=-1)
    u = jax.nn.gelu(u)
    v = layernorm(jax.nn.gelu(v), ln_g, ln_b)
    vr = v.reshape(Bsz, S // CHUNK, CHUNK, GM_GROUPS, GM_GW)
    tril = jnp.tril(jnp.ones((CHUNK, CHUNK), dtype=bool))
    wm = jnp.where(tril, w_s, 0.0)
    sp = jnp.einsum('gij,bnjgc->bnigc', wm, vr) + b_s.T[:, :, None]
    return u * sp.reshape(Bsz, S, MIX_W) * jax.nn.silu(z)


def hybrid_layer(x, c, g_pre, g_post, w_ada, b_ada, w_in, conv_w, conv_b, pos_ck, w_ck1, w_ck2,
                 pos_cv, w_cv1, w_cv2, ln_g, ln_b, w_s, b_s, w_br, w_out):
    shift, scl, gate = jnp.split(jax.nn.silu(c) @ w_ada + b_ada, 3, axis=-1)
    h = rmsnorm(x, g_pre) * (1.0 + scl[:, None]) + shift[:, None]
    ys = (
        short_conv_mixer(h, w_in[:, A_OFF:A_OFF + A_COLS], conv_w, conv_b),
        nsa_mixer(h, w_in[:, B_OFF:B_OFF + B_COLS], pos_ck, w_ck1, w_ck2, pos_cv, w_cv1, w_cv2),
        gmlp_mixer(h, w_in[:, C_OFF:C_OFF + C_COLS], ln_g, ln_b, w_s, b_s),
    )
    merged = None
    for i in range(N_BRANCH):
        g_i = jax.nn.sigmoid(h @ w_in[:, G_OFF + i * D_MODEL:G_OFF + (i + 1) * D_MODEL])
        term = g_i * (ys[i] @ w_br[i])
        merged = term if merged is None else merged + term
    out = rmsnorm(merged @ w_out, g_post)
    return x + gate[:, None] * out


def setup_inputs(seed: int = 0) -> dict:
    key = jax.random.key(seed)
    k = jax.random.split(key, 24)
    nrm = jax.random.normal
    f = jnp.float32
    return {
        "x": nrm(k[0], (BATCH, SEQ, D_MODEL), f),
        "c": nrm(k[1], (BATCH, D_MODEL), f),
        "g_pre": 1.0 + 0.02 * nrm(k[2], (DEPTH, D_MODEL), f),
        "g_post": 1.0 + 0.02 * nrm(k[3], (DEPTH, D_MODEL), f),
        "w_ada": 0.5 * D_MODEL ** -0.5 * nrm(k[4], (DEPTH, D_MODEL, 3 * D_MODEL), f),
        "b_ada": 0.02 * nrm(k[5], (DEPTH, 3 * D_MODEL), f),
        "w_in": D_MODEL ** -0.5 * nrm(k[6], (DEPTH, D_MODEL, IN_COLS), f),
        "conv_w": CONV_K ** -0.5 * nrm(k[7], (DEPTH, CONV_K, MIX_W), f),
        "conv_b": 0.02 * nrm(k[8], (DEPTH, MIX_W), f),
        "pos_ck": 0.02 * nrm(k[9], (DEPTH, L_CMP, HEAD_DIM), f),
        "w_ck1": (L_CMP * HEAD_DIM) ** -0.5 * nrm(k[10], (DEPTH, L_CMP * HEAD_DIM, CMP_HIDDEN), f),
        "w_ck2": CMP_HIDDEN ** -0.5 * nrm(k[11], (DEPTH, CMP_HIDDEN, HEAD_DIM), f),
        "pos_cv": 0.02 * nrm(k[12], (DEPTH, L_CMP, HEAD_DIM), f),
        "w_cv1": (L_CMP * HEAD_DIM) ** -0.5 * nrm(k[13], (DEPTH, L_CMP * HEAD_DIM, CMP_HIDDEN), f),
        "w_cv2": CMP_HIDDEN ** -0.5 * nrm(k[14], (DEPTH, CMP_HIDDEN, HEAD_DIM), f),
        "ln_g": 1.0 + 0.02 * nrm(k[15], (DEPTH, MIX_W), f),
        "ln_b": 0.02 * nrm(k[16], (DEPTH, MIX_W), f),
        "w_s": CHUNK ** -0.5 * nrm(k[17], (DEPTH, GM_GROUPS, CHUNK, CHUNK), f),
        "b_s": 1.0 + 0.02 * nrm(k[18], (DEPTH, GM_GROUPS, CHUNK), f),
        "w_br": MIX_W ** -0.5 * nrm(k[19], (DEPTH, N_BRANCH, MIX_W, D_MODEL), f),
        "w_out": D_MODEL ** -0.5 * nrm(k[20], (DEPTH, D_MODEL, D_MODEL), f),
    }


def reference(x, c, g_pre, g_post, w_ada, b_ada, w_in, conv_w, conv_b, pos_ck, w_ck1, w_ck2,
              pos_cv, w_cv1, w_cv2, ln_g, ln_b, w_s, b_s, w_br, w_out):
    for l in range(DEPTH):
        x = hybrid_layer(x, c, g_pre[l], g_post[l], w_ada[l], b_ada[l], w_in[l], conv_w[l], conv_b[l],
                         pos_ck[l], w_ck1[l], w_ck2[l], pos_cv[l], w_cv1[l], w_cv2[l],
                         ln_g[l], ln_b[l], w_s[l], b_s[l], w_br[l], w_out[l])
    return x
```

```python
import functools

import jax
import jax.numpy as jnp
from jax import lax
from jax.experimental import pallas as pl
from jax.experimental.pallas import tpu as pltpu

D_MODEL = 1024
MIX_W = 1024
CONV_K = 3
N_HEADS = 16
HEAD_DIM = 64
N_KV = 4
HPG = N_HEADS // N_KV
KV_W = N_KV * HEAD_DIM
L_CMP = 32
STRIDE_CMP = 16
CMP_HIDDEN = 128
L_SEL = 64
N_SEL_BLOCKS = 16
WINDOW = 512
CHUNK = 128
GM_GROUPS = 8
N_BRANCH = 3
EPS = 1e-6
NEG = -1e30

A_OFF = 0
A_COLS = 4 * MIX_W
B_OFF = A_OFF + A_COLS
B_COLS = 2 * MIX_W + 6 * KV_W + 3 * N_HEADS
C_OFF = B_OFF + B_COLS
C_COLS = 3 * MIX_W
G_OFF = C_OFF + C_COLS

LANES = 128
PAIR_W = 2 * HEAD_DIM
Q_COL = 0
KC_COL = Q_COL + MIX_W
VC_COL = KC_COL + KV_W
KVS_COL = VC_COL + KV_W
KVW_COL = KVS_COL + N_KV * PAIR_W
Z_COL = KVW_COL + N_KV * PAIR_W
GL_COL = Z_COL + MIX_W
PB_COLS = GL_COL + N_KV * LANES

KEY_CHUNK = 256
Q_TILE = 128
MASK_BIG = 2.0 ** 100
VMEM_LIMIT = 60 * 1024 * 1024

f32 = jnp.float32
bf16 = jnp.bfloat16


def _dot(a, b):
    return jnp.dot(a, b, preferred_element_type=f32)


def _dot_nt(a, b):
    return lax.dot_general(a, b, (((1,), (1,)), ((), ())), preferred_element_type=f32)


def _silu(x):
    return x * jax.nn.sigmoid(x)


def _norm_mod(x, g, scl, shift):
    y = x * lax.rsqrt(jnp.mean(x * x, axis=-1, keepdims=True) + EPS)
    return (y * g) * (1.0 + scl) + shift


def _ada_kernel(c_ref, w_ref, b_ref, o_ref):
    c = c_ref[...]
    o_ref[0] = jnp.dot(_silu(c), w_ref[0], preferred_element_type=f32,
                       precision=lax.Precision.HIGHEST) + b_ref[0]


def _ada(c, w_ada, b_ada):
    depth, d, n = w_ada.shape
    bsz = c.shape[0]
    return pl.pallas_call(
        _ada_kernel,
        grid=(depth,),
        in_specs=[pl.BlockSpec((bsz, d), lambda l: (0, 0)),
                  pl.BlockSpec((1, d, n), lambda l: (l, 0, 0)),
                  pl.BlockSpec((1, 1, n), lambda l: (l, 0, 0))],
        out_specs=pl.BlockSpec((1, bsz, n), lambda l: (l, 0, 0)),
        out_shape=jax.ShapeDtypeStruct((depth, bsz, n), f32),
        compiler_params=pltpu.CompilerParams(vmem_limit_bytes=VMEM_LIMIT),
        name="ada",
    )(c, w_ada, b_ada.reshape(depth, 1, n))


def _proj_kernel(x_ref, mod_ref, g_ref, w_ref, o_ref, *, col_step):
    mod = mod_ref[0]
    h = _norm_mod(x_ref[0], g_ref[...], mod[:, D_MODEL:2 * D_MODEL], mod[:, :D_MODEL]).astype(bf16)
    for c0 in range(0, PB_COLS, col_step):
        o_ref[0, :, c0:c0 + col_step] = _dot(h, w_ref[:, c0:c0 + col_step]).astype(bf16)


def _proj(x, mod, g_pre, wb, tm):
    bsz, s, d = x.shape
    return pl.pallas_call(
        functools.partial(_proj_kernel, col_step=512),
        grid=(bsz, s // tm),
        in_specs=[pl.BlockSpec((1, tm, d), lambda b, i: (b, i, 0)),
                  pl.BlockSpec((1, 1, 3 * d), lambda b, i: (b, 0, 0)),
                  pl.BlockSpec((1, d), lambda b, i: (0, 0)),
                  pl.BlockSpec((d, PB_COLS), lambda b, i: (0, 0))],
        out_specs=pl.BlockSpec((1, tm, PB_COLS), lambda b, i: (b, i, 0)),
        out_shape=jax.ShapeDtypeStruct((bsz, s, PB_COLS), bf16),
        compiler_params=pltpu.CompilerParams(vmem_limit_bytes=VMEM_LIMIT),
        name="proj",
    )(x, mod, g_pre, wb)


def _cmp_kernel(tk_ref, tv_ref, pk_ref, pv_ref, w1k_ref, w1v_ref, w2k_ref, w2v_ref, o_ref):
    half = STRIDE_CMP * HEAD_DIM

    def hidden(t_ref, p_ref, w1_ref):
        t = t_ref[0, 0].astype(f32)
        n_rows = t.shape[0]
        ta = (t + p_ref[0:1, :]).astype(bf16)
        tb = (t + p_ref[1:2, :]).astype(bf16)
        a = _dot(ta, w1_ref[0:half, :])
        b = _dot(tb, w1_ref[half:2 * half, :])
        return _silu(a + pltpu.roll(b, n_rows - 1, axis=0)).astype(bf16)

    hk = hidden(tk_ref, pk_ref, w1k_ref)
    hv = hidden(tv_ref, pv_ref, w1v_ref)
    o_ref[0, 0] = (_dot(hk, w2k_ref[...]) + _dot(hv, w2v_ref[...])).astype(bf16)


def _compress(tk, tv, posk, posv, w1k, w1v, w2k, w2v):
    bsz, g, n_rows, width = tk.shape
    tspec = pl.BlockSpec((1, 1, n_rows, width), lambda b, j: (b, j, 0, 0))

    def full(a):
        return pl.BlockSpec(a.shape, lambda b, j: (0,) * a.ndim)

    return pl.pallas_call(
        _cmp_kernel,
        grid=(bsz, g),
        in_specs=[tspec, tspec, full(posk), full(posv), full(w1k), full(w1v), full(w2k), full(w2v)],
        out_specs=pl.BlockSpec((1, 1, n_rows, PAIR_W), lambda b, j: (b, j, 0, 0)),
        out_shape=jax.ShapeDtypeStruct((bsz, g, n_rows, PAIR_W), bf16),
        compiler_params=pltpu.CompilerParams(vmem_limit_bytes=VMEM_LIMIT),
        name="compress",
    )(tk, tv, posk, posv, w1k, w1v, w2k, w2v)


def _attn_kernel(slopes_ref, q_ref, kvs_ref, kvw_ref, kvc_ref, z_ref, gl_ref, o_ref, *, tq, n_cmp_pad):
    g = pl.program_id(1)
    qt = pl.program_id(2)
    t0 = qt * tq
    rows = HPG * tq
    kc = KEY_CHUNK
    c_hi = t0 // kc

    lane_q = lax.broadcasted_iota(jnp.int32, (tq, LANES), 1)
    lo_q = lane_q < HEAD_DIM

    qf = q_ref[0].astype(f32) * (HEAD_DIM ** -0.5)
    heads = []
    for m in range(HPG // 2):
        slab = qf[:, m * LANES:(m + 1) * LANES]
        heads.append(jnp.where(lo_q, slab, 0.0))
        heads.append(jnp.where(lo_q, pltpu.roll(slab, HEAD_DIM, axis=1), 0.0))
    q_lo = jnp.concatenate(heads, axis=0)
    q_b = q_lo.astype(bf16)

    def per_row(width):
        r = lax.broadcasted_iota(jnp.int32, (rows, width), 0)
        head = r // tq
        slope = jnp.where(head == 0, slopes_ref[g, 0],
                          jnp.where(head == 1, slopes_ref[g, 1],
                                    jnp.where(head == 2, slopes_ref[g, 2], slopes_ref[g, 3])))
        trow = r - head * tq
        return slope, trow

    slope_c, trow_c = per_row(n_cmp_pad)
    slope_k, trow_k = per_row(kc)
    lane_k = lax.broadcasted_iota(jnp.int32, (rows, kc), 1)
    dq_k = trow_k - lane_k

    kvc = kvc_ref[0, 0]
    lane_c = lax.broadcasted_iota(jnp.int32, (rows, n_cmp_pad), 1)
    d_c = (t0 + trow_c) - (lane_c * STRIDE_CMP + (L_CMP - 1))
    mask_c = d_c >= 0
    s = _dot_nt(q_b, kvc) - slope_c * d_c.astype(f32)
    s = jnp.where(mask_c, s, NEG)
    e = jnp.where(mask_c, jnp.exp(s - jnp.max(s, axis=-1, keepdims=True)), 0.0)
    l_c = jnp.sum(e, axis=-1, keepdims=True)
    p_c = e * (1.0 / jnp.where(l_c > 0.0, l_c, 1.0))
    o_cmp = _dot(p_c.astype(bf16), kvc)

    p_sum = p_c[0:tq] + p_c[tq:2 * tq] + p_c[2 * tq:3 * tq] + p_c[3 * tq:4 * tq]
    p_hi = p_sum.astype(bf16)
    r1 = p_sum - p_hi.astype(f32)
    p_mid = r1.astype(bf16)
    p_lo = (r1 - p_mid.astype(f32)).astype(bf16)
    blk_r = lax.broadcasted_iota(jnp.int32, (LANES, n_cmp_pad), 0) - HEAD_DIM
    cmp_c = lax.broadcasted_iota(jnp.int32, (LANES, n_cmp_pad), 1)
    ratio = L_SEL // STRIDE_CMP
    ovl_t = jnp.where((blk_r >= 0) & (cmp_c >= ratio * blk_r - (L_CMP // STRIDE_CMP - 1))
                      & (cmp_c <= ratio * blk_r + ratio - 1), 1.0, 0.0).astype(bf16)
    imp_t = _dot_nt(ovl_t, p_hi) + _dot_nt(ovl_t, p_mid) + _dot_nt(ovl_t, p_lo)
    n_blk = LANES - HEAD_DIM
    imp = jnp.maximum(imp_t[HEAD_DIM:, :], 0.0)
    srow = lax.broadcasted_iota(jnp.int32, (n_blk, tq), 0)
    cur = (t0 + lax.broadcasted_iota(jnp.int32, (n_blk, tq), 1)) // L_SEL
    forced = (srow == 0) | (srow == cur) | (srow == cur - 1)
    work = jnp.where((srow >= 1) & (srow <= cur - 2), imp, -1.0)
    sel = jnp.where(forced, 1.0, 0.0)
    srow_f = srow.astype(f32)
    for _ in range(N_SEL_BLOCKS - 3):
        best = jnp.max(work, axis=0, keepdims=True)
        first = jnp.min(jnp.where(work == best, srow_f, float(n_blk)), axis=0, keepdims=True)
        pick = (srow_f == first) & (best >= 0.0)
        sel = jnp.where(pick, 1.0, sel)
        work = jnp.where(pick, -1.0, work)
    pen_t = jnp.concatenate([jnp.zeros((HEAD_DIM, tq), f32), (sel - 1.0) * MASK_BIG], axis=0)
    pen = pen_t.T
    qs_b = (q_lo + jnp.concatenate([pen] * HPG, axis=0)).astype(bf16)

    lane_e = lax.broadcasted_iota(jnp.int32, (kc, LANES), 1)
    row_e = lax.broadcasted_iota(jnp.int32, (kc, LANES), 0)
    lo_e = lane_e < HEAD_DIM

    def softmax_step(carry, s_chunk, kv):
        m_prev, l_prev, acc = carry
        m_new = jnp.maximum(m_prev, jnp.max(s_chunk, axis=-1, keepdims=True))
        alpha = jnp.exp(m_prev - m_new)
        p = jnp.exp(s_chunk - m_new)
        l_new = alpha * l_prev + jnp.sum(p, axis=-1, keepdims=True)
        return m_new, l_new, alpha * acc + _dot(p.astype(bf16), kv)

    def first_step(s_chunk, kv):
        m = jnp.max(s_chunk, axis=-1, keepdims=True)
        p = jnp.exp(s_chunk - m)
        return m, jnp.sum(p, axis=-1, keepdims=True), _dot(p.astype(bf16), kv)

    def load_chunk(ref, c):
        return ref[0, pl.ds(pl.multiple_of(c * kc, kc), kc), :]

    def slc_scores(c, kv):
        blk_e = (c * (kc // L_SEL) + row_e // L_SEL) + HEAD_DIM
        k_e = jnp.where(lo_e, kv, jnp.where(lane_e == blk_e, 1.0, 0.0).astype(bf16))
        d = dq_k + (t0 - c * kc)
        return _dot_nt(qs_b, k_e) - slope_k * d.astype(f32), d

    kv = load_chunk(kvs_ref, c_hi)
    s, d = slc_scores(c_hi, kv)
    carry = first_step(jnp.where(d >= 0, s, NEG), kv)

    def slc_body(c, carry):
        kv = load_chunk(kvs_ref, c)
        s, _ = slc_scores(c, kv)
        return softmax_step(carry, s, kv)

    _, l_s, acc_s = lax.fori_loop(0, c_hi, slc_body, carry)
    o_slc = acc_s * (1.0 / l_s)

    def win_scores(c, kv):
        d = dq_k + (t0 - c * kc)
        s = _dot_nt(q_b, kv) - slope_k * d.astype(f32)
        return jnp.where((d >= 0) & (d < WINDOW), s, NEG)

    kv = load_chunk(kvw_ref, c_hi)
    carry = first_step(win_scores(c_hi, kv), kv)

    def win_body(c, carry):
        kv = load_chunk(kvw_ref, c)
        return softmax_step(carry, win_scores(c, kv), kv)

    n_back = (WINDOW + kc - 1) // kc
    _, l_w, acc_w = lax.fori_loop(jnp.maximum(c_hi - n_back, 0), c_hi, win_body, carry)
    o_win = acc_w * (1.0 / l_w)

    gates = jax.nn.sigmoid(gl_ref[0].astype(f32))
    outs = []
    for n in range(HPG):
        r0 = n * tq
        outs.append(gates[:, 3 * n:3 * n + 1] * o_cmp[r0:r0 + tq]
                    + gates[:, 3 * n + 1:3 * n + 2] * o_slc[r0:r0 + tq]
                    + gates[:, 3 * n + 2:3 * n + 3] * o_win[r0:r0 + tq])
    slabs = [jnp.where(lo_q, pltpu.roll(outs[2 * m], HEAD_DIM, axis=1), outs[2 * m + 1])
             for m in range(HPG // 2)]
    o = jnp.concatenate(slabs, axis=1)
    o_ref[0] = (o * _silu(z_ref[0].astype(f32))).astype(bf16)


def _attention(slopes, pb, kvc, tq):
    bsz, s, _ = pb.shape
    n_cmp_pad = kvc.shape[2]
    grp_w = HPG * HEAD_DIM
    return pl.pallas_call(
        functools.partial(_attn_kernel, tq=tq, n_cmp_pad=n_cmp_pad),
        grid=(bsz, N_KV, s // tq),
        in_specs=[pl.BlockSpec(memory_space=pltpu.SMEM),
                  pl.BlockSpec((1, tq, grp_w), lambda b, g, i: (b, i, Q_COL // grp_w + g)),
                  pl.BlockSpec((1, s, PAIR_W), lambda b, g, i: (b, 0, KVS_COL // PAIR_W + g)),
                  pl.BlockSpec((1, s, PAIR_W), lambda b, g, i: (b, 0, KVW_COL // PAIR_W + g)),
                  pl.BlockSpec((1, 1, n_cmp_pad, PAIR_W), lambda b, g, i: (b, g, 0, 0)),
                  pl.BlockSpec((1, tq, grp_w), lambda b, g, i: (b, i, Z_COL // grp_w + g)),
                  pl.BlockSpec((1, tq, LANES), lambda b, g, i: (b, i, GL_COL // LANES + g))],
        out_specs=pl.BlockSpec((1, tq, grp_w), lambda b, g, i: (b, i, g)),
        out_shape=jax.ShapeDtypeStruct((bsz, s, MIX_W), bf16),
        compiler_params=pltpu.CompilerParams(vmem_limit_bytes=VMEM_LIMIT),
        name="attn",
    )(slopes, pb, pb, pb, kvc, pb, pb)


def _post_kernel(x_ref, xh_ref, yb_ref, mod_ref, gpre_ref, gpost_ref, wa_ref, wc_ref, wg_ref, wbr_ref, wout_ref,
                 convw_ref, convb_ref, lng_ref, lnb_ref, ws_ref, bs_ref, o_ref, *, tm, cw):
    i = pl.program_id(1)
    mod = mod_ref[0]
    shift, scl, gate = mod[:, :D_MODEL], mod[:, D_MODEL:2 * D_MODEL], mod[:, 2 * D_MODEL:]
    x = x_ref[0]
    h = _norm_mod(x, gpre_ref[...], scl, shift).astype(bf16)
    hh = _norm_mod(xh_ref[0], gpre_ref[...], scl, shift).astype(bf16)
    row = lax.broadcasted_iota(jnp.int32, (tm, cw), 0)
    n_cc = MIX_W // cw

    acc_a = jnp.zeros((tm, D_MODEL), f32)
    for j in range(n_cc):
        def col(k):
            return slice(k * MIX_W + j * cw, k * MIX_W + (j + 1) * cw)
        b_ = _dot(h, wa_ref[:, col(0)])
        y = _dot(h, wa_ref[:, col(1)]) * _dot(h, wa_ref[:, col(2)])
        z = _dot(h, wa_ref[:, col(3)])
        yh = _dot(hh, wa_ref[:, col(1)]) * _dot(hh, wa_ref[:, col(2)])
        yh = jnp.where(i > 0, yh, 0.0)
        y1 = jnp.where(row == 0, yh[7:8, :], pltpu.roll(y, 1, axis=0))
        y2 = jnp.where(row == 0, yh[6:7, :], jnp.where(row == 1, yh[7:8, :], pltpu.roll(y, 2, axis=0)))
        cs = slice(j * cw, (j + 1) * cw)
        conv = (convb_ref[:, cs] + convw_ref[0:1, cs] * y2 + convw_ref[1:2, cs] * y1
                + convw_ref[2:3, cs] * y)
        ya = (b_ * conv * _silu(z)).astype(bf16)
        acc_a = acc_a + _dot(ya, wbr_ref[0, cs, :])
    merged = jax.nn.sigmoid(_dot(h, wg_ref[:, 0:D_MODEL])) * acc_a

    merged = merged + jax.nn.sigmoid(_dot(h, wg_ref[:, D_MODEL:2 * D_MODEL])) * _dot(yb_ref[0], wbr_ref[1])

    v = jax.nn.gelu(_dot(h, wc_ref[:, MIX_W:2 * MIX_W]))
    mu = jnp.mean(v, axis=-1, keepdims=True)
    vc = v - mu
    vn = (vc * lax.rsqrt(jnp.mean(vc * vc, axis=-1, keepdims=True) + EPS)) * lng_ref[...] + lnb_ref[...]
    vn = vn.astype(bf16)
    ri = lax.broadcasted_iota(jnp.int32, (CHUNK, CHUNK), 0)
    ci = lax.broadcasted_iota(jnp.int32, (CHUNK, CHUNK), 1)
    acc_c = jnp.zeros((tm, D_MODEL), f32)
    gw = MIX_W // GM_GROUPS
    for gi in range(GM_GROUPS):
        cs = slice(gi * gw, (gi + 1) * gw)
        wm = jnp.where(ri >= ci, ws_ref[gi], 0.0).astype(bf16)
        sp = jnp.concatenate([_dot(wm, vn[k * CHUNK:(k + 1) * CHUNK, cs]) for k in range(tm // CHUNK)], axis=0)
        sp = sp + jnp.concatenate([bs_ref[:, cs]] * (tm // CHUNK), axis=0)
        u = jax.nn.gelu(_dot(h, wc_ref[:, cs]))
        z = _dot(h, wc_ref[:, 2 * MIX_W + gi * gw:2 * MIX_W + (gi + 1) * gw])
        yc = (u * sp * _silu(z)).astype(bf16)
        acc_c = acc_c + _dot(yc, wbr_ref[2, cs, :])
    merged = merged + jax.nn.sigmoid(_dot(h, wg_ref[:, 2 * D_MODEL:3 * D_MODEL])) * acc_c

    o = _dot(merged.astype(bf16), wout_ref[...])
    o = o * lax.rsqrt(jnp.mean(o * o, axis=-1, keepdims=True) + EPS) * gpost_ref[...]
    o_ref[0] = x + gate * o


def _post(x, yb, mod, g_pre, g_post, wa, wc, wg, wbr, wout, conv_w, conv_b, ln_g, ln_b, w_s, bs_full, tm):
    bsz, s, d = x.shape
    halo = 8

    def const(a):
        return pl.BlockSpec(a.shape, lambda b, i: (0,) * a.ndim, pipeline_mode=pl.Buffered(1))

    consts = (g_pre, g_post, wa, wc, wg, wbr, wout, conv_w, conv_b, ln_g, ln_b, w_s, bs_full)
    return pl.pallas_call(
        functools.partial(_post_kernel, tm=tm, cw=256),
        grid=(bsz, s // tm),
        in_specs=[pl.BlockSpec((1, tm, d), lambda b, i: (b, i, 0)),
                  pl.BlockSpec((1, halo, d), lambda b, i: (b, jnp.maximum(i * (tm // halo) - 1, 0), 0)),
                  pl.BlockSpec((1, tm, MIX_W), lambda b, i: (b, i, 0)),
                  pl.BlockSpec((1, 1, 3 * d), lambda b, i: (b, 0, 0))] + [const(a) for a in consts],
        out_specs=pl.BlockSpec((1, tm, d), lambda b, i: (b, i, 0)),
        out_shape=jax.ShapeDtypeStruct((bsz, s, d), f32),
        compiler_params=pltpu.CompilerParams(vmem_limit_bytes=VMEM_LIMIT),
        name="post",
    )(x, x, yb, mod, *consts)


def _pack_pairs(a, b):
    d = a.shape[0]
    return jnp.concatenate([a.reshape(d, N_KV, HEAD_DIM), b.reshape(d, N_KV, HEAD_DIM)], axis=-1).reshape(d, -1)


def _attn_weight(w_in_b):
    d = w_in_b.shape[0]
    sizes = [MIX_W] + [KV_W] * 6 + [MIX_W, 3 * N_HEADS]
    offs = [0]
    for sz in sizes:
        offs.append(offs[-1] + sz)
    q, kc, vc, ks, vs, kw, vw, z, gl = [w_in_b[:, offs[k]:offs[k + 1]] for k in range(len(sizes))]
    gl = jnp.pad(gl.reshape(d, N_KV, HPG * 3), ((0, 0), (0, 0), (0, LANES - HPG * 3))).reshape(d, N_KV * LANES)
    return jnp.concatenate([q, kc, vc, _pack_pairs(ks, vs), _pack_pairs(kw, vw), z, gl], axis=1).astype(bf16)


def _block_rows(t, s):
    bsz = t.shape[0]
    t = t.reshape(bsz, s // STRIDE_CMP, STRIDE_CMP, N_KV, HEAD_DIM).transpose(0, 3, 1, 2, 4)
    return t.reshape(bsz, N_KV, s // STRIDE_CMP, STRIDE_CMP * HEAD_DIM)


def _layer(x, mod, slopes, g_pre, g_post, w_in, conv_w, conv_b, pos_ck, w_ck1, w_ck2, pos_cv, w_cv1, w_cv2,
           ln_g, ln_b, w_s, b_s, w_br, w_out):
    bsz, s, d = x.shape
    assert s % KEY_CHUNK == 0 and s // L_SEL <= LANES - HEAD_DIM and s % 512 == 0
    row = lambda a: a.reshape(1, -1)
    pb = _proj(x, mod, row(g_pre), _attn_weight(w_in[:, B_OFF:B_OFF + B_COLS]), tm=512)

    zpad = jnp.zeros((CMP_HIDDEN, HEAD_DIM), f32)
    kvc = _compress(
        _block_rows(pb[:, :, KC_COL:KC_COL + KV_W], s), _block_rows(pb[:, :, VC_COL:VC_COL + KV_W], s),
        pos_ck.reshape(2, -1), pos_cv.reshape(2, -1), w_ck1.astype(bf16), w_cv1.astype(bf16),
        jnp.concatenate([w_ck2, zpad], axis=1).astype(bf16), jnp.concatenate([zpad, w_cv2], axis=1).astype(bf16))
    yb = _attention(slopes, pb, kvc, tq=Q_TILE)

    bs_full = jnp.repeat(b_s.T, MIX_W // GM_GROUPS, axis=1)
    return _post(x, yb, mod, row(g_pre), row(g_post),
                 w_in[:, A_OFF:A_OFF + A_COLS].astype(bf16), w_in[:, C_OFF:C_OFF + C_COLS].astype(bf16),
                 w_in[:, G_OFF:G_OFF + N_BRANCH * D_MODEL].astype(bf16), w_br.astype(bf16), w_out.astype(bf16),
                 conv_w, row(conv_b), row(ln_g), row(ln_b), w_s, bs_full, tm=256)


def kernel(x, c, g_pre, g_post, w_ada, b_ada, w_in, conv_w, conv_b, pos_ck, w_ck1, w_ck2, pos_cv, w_cv1, w_cv2,
           ln_g, ln_b, w_s, b_s, w_br, w_out):
    depth = w_in.shape[0]
    bsz = x.shape[0]
    mods = _ada(c, w_ada, b_ada).reshape(depth, bsz, 1, 3 * D_MODEL)
    head = jnp.arange(1, N_HEADS + 1, dtype=f32)
    slopes = (2.0 ** (-8.0 * head / N_HEADS)).reshape(N_KV, HPG)
    for l in range(depth):
        x = _layer(x, mods[l], slopes, g_pre[l], g_post[l], w_in[l], conv_w[l], conv_b[l], pos_ck[l], w_ck1[l],
                   w_ck2[l], pos_cv[l], w_cv1[l], w_cv2[l], ln_g[l], ln_b[l], w_s[l], b_s[l], w_br[l], w_out[l])
    return x
```

```python
import functools

import jax
import jax.numpy as jnp
from jax import lax
from jax.experimental import pallas as pl
from jax.experimental.pallas import tpu as pltpu

D_MODEL = 1024
MIX_W = 1024
CONV_K = 3
N_HEADS = 16
HEAD_DIM = 64
N_KV = 4
HPG = N_HEADS // N_KV
KV_W = N_KV * HEAD_DIM
L_CMP = 32
STRIDE_CMP = 16
CMP_HIDDEN = 128
L_SEL = 64
N_SEL_BLOCKS = 16
WINDOW = 512
CHUNK = 128
GM_GROUPS = 8
N_BRANCH = 3
EPS = 1e-6
NEG = -1e30

A_OFF = 0
A_COLS = 4 * MIX_W
B_OFF = A_OFF + A_COLS
B_COLS = 2 * MIX_W + 6 * KV_W + 3 * N_HEADS
C_OFF = B_OFF + B_COLS
C_COLS = 3 * MIX_W
G_OFF = C_OFF + C_COLS

LANES = 128
PAIR_W = 2 * HEAD_DIM
Q_COL = 0
KC_COL = Q_COL + MIX_W
VC_COL = KC_COL + KV_W
KVS_COL = VC_COL + KV_W
KVW_COL = KVS_COL + N_KV * PAIR_W
Z_COL = KVW_COL + N_KV * PAIR_W
GL_COL = Z_COL + MIX_W
PB_COLS = GL_COL + N_KV * LANES

KEY_CHUNK = 512
Q_TILE = 256
MASK_BIG = 2.0 ** 100
LOG2E = 1.4426950408889634
VMEM_LIMIT = 60 * 1024 * 1024

f32 = jnp.float32
bf16 = jnp.bfloat16


def _dot(a, b):
    return jnp.dot(a, b, preferred_element_type=f32)


def _silu(x):
    return x * jax.nn.sigmoid(x)


def _norm_mod(x, g, scl, shift):
    y = x * lax.rsqrt(jnp.mean(x * x, axis=-1, keepdims=True) + EPS)
    return (y * g) * (1.0 + scl) + shift


def _ada_kernel(c_ref, w_ref, b_ref, o_ref):
    c = c_ref[...]
    o_ref[0] = jnp.dot(_silu(c), w_ref[0], preferred_element_type=f32,
                       precision=lax.Precision.HIGHEST) + b_ref[0]


def _ada(c, w_ada, b_ada):
    depth, d, n = w_ada.shape
    bsz = c.shape[0]
    return pl.pallas_call(
        _ada_kernel,
        grid=(depth,),
        in_specs=[pl.BlockSpec((bsz, d), lambda l: (0, 0)),
                  pl.BlockSpec((1, d, n), lambda l: (l, 0, 0)),
                  pl.BlockSpec((1, 1, n), lambda l: (l, 0, 0))],
        out_specs=pl.BlockSpec((1, bsz, n), lambda l: (l, 0, 0)),
        out_shape=jax.ShapeDtypeStruct((depth, bsz, n), f32),
        compiler_params=pltpu.CompilerParams(vmem_limit_bytes=VMEM_LIMIT),
        name="ada",
    )(c, w_ada, b_ada.reshape(depth, 1, n))


def _proj_kernel(x_ref, mod_ref, g_ref, w_ref, o_ref, *, col_step):
    mod = mod_ref[0]
    h = _norm_mod(x_ref[0], g_ref[...], mod[:, D_MODEL:2 * D_MODEL], mod[:, :D_MODEL]).astype(bf16)
    for c0 in range(0, PB_COLS, col_step):
        o_ref[0, :, c0:c0 + col_step] = _dot(h, w_ref[:, c0:c0 + col_step]).astype(bf16)


def _proj(x, mod, g_pre, wb, tm):
    bsz, s, d = x.shape
    return pl.pallas_call(
        functools.partial(_proj_kernel, col_step=512),
        grid=(bsz, s // tm),
        in_specs=[pl.BlockSpec((1, tm, d), lambda b, i: (b, i, 0)),
                  pl.BlockSpec((1, 1, 3 * d), lambda b, i: (b, 0, 0)),
                  pl.BlockSpec((1, d), lambda b, i: (0, 0)),
                  pl.BlockSpec((d, PB_COLS), lambda b, i: (0, 0))],
        out_specs=pl.BlockSpec((1, tm, PB_COLS), lambda b, i: (b, i, 0)),
        out_shape=jax.ShapeDtypeStruct((bsz, s, PB_COLS), bf16),
        compiler_params=pltpu.CompilerParams(vmem_limit_bytes=VMEM_LIMIT),
        name="proj",
    )(x, mod, g_pre, wb)


def _cmp_kernel(tk_ref, tv_ref, pk_ref, pv_ref, w1k_ref, w1v_ref, w2k_ref, w2v_ref, o_ref):
    half = STRIDE_CMP * HEAD_DIM

    def hidden(t_ref, p_ref, w1_ref):
        t = t_ref[0, 0].astype(f32)
        n_rows = t.shape[0]
        ta = (t + p_ref[0:1, :]).astype(bf16)
        tb = (t + p_ref[1:2, :]).astype(bf16)
        a = _dot(ta, w1_ref[0:half, :])
        b = _dot(tb, w1_ref[half:2 * half, :])
        return _silu(a + pltpu.roll(b, n_rows - 1, axis=0)).astype(bf16)

    hk = hidden(tk_ref, pk_ref, w1k_ref)
    hv = hidden(tv_ref, pv_ref, w1v_ref)
    o_ref[0, 0] = (_dot(hk, w2k_ref[...]) + _dot(hv, w2v_ref[...])).astype(bf16)


def _compress(tk, tv, posk, posv, w1k, w1v, w2k, w2v):
    bsz, g, n_rows, width = tk.shape
    tspec = pl.BlockSpec((1, 1, n_rows, width), lambda b, j: (b, j, 0, 0))

    def full(a):
        return pl.BlockSpec(a.shape, lambda b, j: (0,) * a.ndim)

    return pl.pallas_call(
        _cmp_kernel,
        grid=(bsz, g),
        in_specs=[tspec, tspec, full(posk), full(posv), full(w1k), full(w1v), full(w2k), full(w2v)],
        out_specs=pl.BlockSpec((1, 1, n_rows, PAIR_W), lambda b, j: (b, j, 0, 0)),
        out_shape=jax.ShapeDtypeStruct((bsz, g, n_rows, PAIR_W), bf16),
        compiler_params=pltpu.CompilerParams(vmem_limit_bytes=VMEM_LIMIT),
        name="compress",
    )(tk, tv, posk, posv, w1k, w1v, w2k, w2v)


def _attn_kernel(slopes_ref, q_ref, kvs_ref, kvst_ref, kvw_ref, kvwt_ref, kvc_ref, kvct_ref, kfeat_ref, cfeat_ref,
                 z_ref, gl_ref, o_ref, *, tq, n_cmp_pad):
    g = pl.program_id(1)
    qt = pl.program_id(2)
    t0 = qt * tq
    cols = HPG * tq
    kc = KEY_CHUNK
    c_hi = t0 // kc

    qf = q_ref[0].astype(f32) * (HEAD_DIM ** -0.5 * LOG2E)
    qts = []
    for m in range(HPG // 2):
        t = qf[:, m * LANES:(m + 1) * LANES].T
        qts += [t[:HEAD_DIM], t[HEAD_DIM:]]
    q_t = jnp.concatenate(qts, axis=1)

    lane = lax.broadcasted_iota(jnp.int32, (1, cols), 1)
    head = lane // tq
    trow = lane - head * tq
    slope = LOG2E * jnp.where(head == 0, slopes_ref[g, 0],
                              jnp.where(head == 1, slopes_ref[g, 1],
                                        jnp.where(head == 2, slopes_ref[g, 2], slopes_ref[g, 3])))
    s_hi = slope.astype(bf16).astype(f32)
    s_mid = (slope - s_hi).astype(bf16).astype(f32)
    s_lo = slope - s_hi - s_mid
    r8 = lax.broadcasted_iota(jnp.int32, (8, cols), 0)
    blk = float(L_SEL)
    feat = jnp.where(r8 == 0, blk * s_hi, jnp.where(r8 == 1, blk * s_mid, jnp.where(r8 == 2, blk * s_lo,
           jnp.where(r8 == 3, s_hi, jnp.where(r8 == 4, s_mid, jnp.where(r8 == 5, s_lo, 0.0))))))
    zeros_h = jnp.zeros((HEAD_DIM, cols), f32)
    top = jnp.concatenate([q_t, zeros_h, feat, jnp.zeros((HEAD_DIM - 8, cols), f32)], axis=0)
    qw_t = jnp.concatenate([top, zeros_h], axis=0).astype(bf16)

    kvc = kvc_ref[0, 0]
    s = _dot(jnp.concatenate([kvc, cfeat_ref[...]], axis=1), qw_t)
    rc = lax.broadcasted_iota(jnp.int32, (n_cmp_pad, cols), 0)
    mask_c = (t0 + trow) >= rc * STRIDE_CMP + (L_CMP - 1)
    s = jnp.where(mask_c, s, NEG)
    e = jnp.where(mask_c, jnp.exp2(s - jnp.max(s, axis=0, keepdims=True)), 0.0)
    l_c = jnp.sum(e, axis=0, keepdims=True)
    p_c = e * (1.0 / jnp.where(l_c > 0.0, l_c, 1.0))
    o_cmp = _dot(kvct_ref[0, 0], p_c.astype(bf16))[HEAD_DIM:]

    dq = trow - lax.broadcasted_iota(jnp.int32, (kc, cols), 0)

    def load(c, ref, reft):
        k0 = pl.multiple_of(c * kc, kc)
        k_aug = jnp.concatenate([ref[0, pl.ds(k0, kc), :], kfeat_ref[pl.ds(k0, kc), :]], axis=1)
        return k_aug, reft[0, :, pl.ds(k0, kc)]

    def softmax_step(carry, s_chunk, kvt):
        m_prev, l_prev, acc = carry
        m_new = jnp.maximum(m_prev, jnp.max(s_chunk, axis=0, keepdims=True))
        alpha = jnp.exp2(m_prev - m_new)
        p = jnp.exp2(s_chunk - m_new)
        l_new = alpha * l_prev + jnp.sum(p, axis=0, keepdims=True)
        return m_new, l_new, alpha * acc + _dot(kvt, p.astype(bf16))

    def first_step(s_chunk, kvt):
        m = jnp.max(s_chunk, axis=0, keepdims=True)
        p = jnp.exp2(s_chunk - m)
        return m, jnp.sum(p, axis=0, keepdims=True), _dot(kvt, p.astype(bf16))

    def finish(carry):
        _, l, acc = carry
        return acc[HEAD_DIM:] * (1.0 / l)

    def win_scores(c, k_aug):
        d = dq + (t0 - c * kc)
        return jnp.where((d >= 0) & (d < WINDOW), _dot(k_aug, qw_t), NEG)

    k_aug, kvt = load(c_hi, kvw_ref, kvwt_ref)
    win_carry = first_step(win_scores(c_hi, k_aug), kvt)

    p_sum = p_c[:, 0:tq] + p_c[:, tq:2 * tq] + p_c[:, 2 * tq:3 * tq] + p_c[:, 3 * tq:4 * tq]
    p_hi = p_sum.astype(bf16)
    r1 = p_sum - p_hi.astype(f32)
    p_mid = r1.astype(bf16)
    p_lo = (r1 - p_mid.astype(f32)).astype(bf16)
    n_blk = HEAD_DIM
    blk_r = lax.broadcasted_iota(jnp.int32, (n_blk, n_cmp_pad), 0)
    cmp_c = lax.broadcasted_iota(jnp.int32, (n_blk, n_cmp_pad), 1)
    ratio = L_SEL // STRIDE_CMP
    ovl = jnp.where((cmp_c >= ratio * blk_r - (L_CMP // STRIDE_CMP - 1)) & (cmp_c <= ratio * blk_r + ratio - 1),
                    1.0, 0.0).astype(bf16)
    imp = jnp.maximum(_dot(ovl, p_hi) + _dot(ovl, p_mid) + _dot(ovl, p_lo), 0.0)
    srow = lax.broadcasted_iota(jnp.int32, (n_blk, tq), 0)
    cur = (t0 + lax.broadcasted_iota(jnp.int32, (n_blk, tq), 1)) // L_SEL
    forced = (srow == 0) | (srow == cur) | (srow == cur - 1)
    work = jnp.where((srow >= 1) & (srow <= cur - 2), imp, -1.0)
    sel = jnp.where(forced, 1.0, 0.0)
    srow_f = srow.astype(f32)
    for _ in range(N_SEL_BLOCKS - 3):
        best = jnp.max(work, axis=0, keepdims=True)
        first = jnp.min(jnp.where(work == best, srow_f, float(n_blk)), axis=0, keepdims=True)
        pick = (srow_f == first) & (best >= 0.0)
        sel = jnp.where(pick, 1.0, sel)
        work = jnp.where(pick, -1.0, work)
    pen = (sel - 1.0) * MASK_BIG
    qs_t = jnp.concatenate([top, jnp.concatenate([pen] * HPG, axis=1)], axis=0).astype(bf16)

    def win_body(c, carry):
        k_aug, kvt = load(c, kvw_ref, kvwt_ref)
        return softmax_step(carry, win_scores(c, k_aug), kvt)

    n_back = (WINDOW + kc - 1) // kc
    o_win = finish(lax.fori_loop(jnp.maximum(c_hi - n_back, 0), c_hi, win_body, win_carry))

    k_aug, kvt = load(c_hi, kvs_ref, kvst_ref)
    d = dq + (t0 - c_hi * kc)
    carry = first_step(jnp.where(d >= 0, _dot(k_aug, qs_t), NEG), kvt)

    def slc_body(c, carry):
        k_aug, kvt = load(c, kvs_ref, kvst_ref)
        return softmax_step(carry, _dot(k_aug, qs_t), kvt)

    o_slc = finish(lax.fori_loop(0, c_hi, slc_body, carry))

    gates_t = jax.nn.sigmoid(gl_ref[0].astype(f32)).T
    outs = []
    for n in range(HPG):
        cs = slice(n * tq, (n + 1) * tq)
        outs.append(gates_t[3 * n:3 * n + 1] * o_cmp[:, cs] + gates_t[3 * n + 1:3 * n + 2] * o_slc[:, cs]
                    + gates_t[3 * n + 2:3 * n + 3] * o_win[:, cs])
    slabs = [jnp.concatenate(outs[2 * m:2 * m + 2], axis=0).T for m in range(HPG // 2)]
    o = jnp.concatenate(slabs, axis=1)
    o_ref[0] = (o * _silu(z_ref[0].astype(f32))).astype(bf16)


def _key_features(pos):
    hi, lo = (pos // L_SEL).astype(f32), (pos % L_SEL).astype(f32)
    lanes = jnp.arange(LANES)
    feat = jnp.where(lanes[None, :] < 3, hi[:, None], jnp.where(lanes[None, :] < 6, lo[:, None], 0.0))
    onehot = (lanes[None, :] - HEAD_DIM == (pos // L_SEL)[:, None]).astype(f32)
    return (feat + onehot).astype(bf16)


def _attention(slopes, pb, kvc, tq):
    bsz, s, _ = pb.shape
    n_cmp_pad = kvc.shape[2]
    grp_w = HPG * HEAD_DIM
    pair_cols = N_KV * PAIR_W
    kvst = pb[:, :, KVS_COL:KVS_COL + pair_cols].transpose(0, 2, 1)
    kvwt = pb[:, :, KVW_COL:KVW_COL + pair_cols].transpose(0, 2, 1)
    kvct = kvc.transpose(0, 1, 3, 2)
    kfeat = _key_features(jnp.arange(s))
    cpos = jnp.arange(n_cmp_pad) * STRIDE_CMP + (L_CMP - 1)
    cfeat = jnp.where(jnp.arange(LANES)[None, :] < HEAD_DIM, _key_features(cpos), 0).astype(bf16)
    return pl.pallas_call(
        functools.partial(_attn_kernel, tq=tq, n_cmp_pad=n_cmp_pad),
        grid=(bsz, N_KV, s // tq),
        in_specs=[pl.BlockSpec(memory_space=pltpu.SMEM),
                  pl.BlockSpec((1, tq, grp_w), lambda b, g, i: (b, i, Q_COL // grp_w + g)),
                  pl.BlockSpec((1, s, PAIR_W), lambda b, g, i: (b, 0, KVS_COL // PAIR_W + g)),
                  pl.BlockSpec((1, PAIR_W, s), lambda b, g, i: (b, g, 0)),
                  pl.BlockSpec((1, s, PAIR_W), lambda b, g, i: (b, 0, KVW_COL // PAIR_W + g)),
                  pl.BlockSpec((1, PAIR_W, s), lambda b, g, i: (b, g, 0)),
                  pl.BlockSpec((1, 1, n_cmp_pad, PAIR_W), lambda b, g, i: (b, g, 0, 0)),
                  pl.BlockSpec((1, 1, PAIR_W, n_cmp_pad), lambda b, g, i: (b, g, 0, 0)),
                  pl.BlockSpec((s, LANES), lambda b, g, i: (0, 0)),
                  pl.BlockSpec((n_cmp_pad, LANES), lambda b, g, i: (0, 0)),
                  pl.BlockSpec((1, tq, grp_w), lambda b, g, i: (b, i, Z_COL // grp_w + g)),
                  pl.BlockSpec((1, tq, LANES), lambda b, g, i: (b, i, GL_COL // LANES + g))],
        out_specs=pl.BlockSpec((1, tq, grp_w), lambda b, g, i: (b, i, g)),
        out_shape=jax.ShapeDtypeStruct((bsz, s, MIX_W), bf16),
        compiler_params=pltpu.CompilerParams(vmem_limit_bytes=VMEM_LIMIT),
        name="attn",
    )(slopes, pb, pb, kvst, pb, kvwt, kvc, kvct, kfeat, cfeat, pb, pb)


def _post_kernel(x_ref, xh_ref, yb_ref, mod_ref, gpre_ref, gpost_ref, wa_ref, wc_ref, wg_ref, wbr_ref, wout_ref,
                 convw_ref, convb_ref, lng_ref, lnb_ref, ws_ref, bs_ref, o_ref, *, tm, cw):
    i = pl.program_id(1)
    mod = mod_ref[0]
    shift, scl, gate = mod[:, :D_MODEL], mod[:, D_MODEL:2 * D_MODEL], mod[:, 2 * D_MODEL:]
    x = x_ref[0]
    h = _norm_mod(x, gpre_ref[...], scl, shift).astype(bf16)
    hh = _norm_mod(xh_ref[0], gpre_ref[...], scl, shift).astype(bf16)
    row = lax.broadcasted_iota(jnp.int32, (tm, cw), 0)
    n_cc = MIX_W // cw

    acc_a = jnp.zeros((tm, D_MODEL), f32)
    for j in range(n_cc):
        def col(k):
            return slice(k * MIX_W + j * cw, k * MIX_W + (j + 1) * cw)
        b_ = _dot(h, wa_ref[:, col(0)])
        y = _dot(h, wa_ref[:, col(1)]) * _dot(h, wa_ref[:, col(2)])
        z = _dot(h, wa_ref[:, col(3)])
        yh = _dot(hh, wa_ref[:, col(1)]) * _dot(hh, wa_ref[:, col(2)])
        yh = jnp.where(i > 0, yh, 0.0)
        y1 = jnp.where(row == 0, yh[7:8, :], pltpu.roll(y, 1, axis=0))
        y2 = jnp.where(row == 0, yh[6:7, :], jnp.where(row == 1, yh[7:8, :], pltpu.roll(y, 2, axis=0)))
        cs = slice(j * cw, (j + 1) * cw)
        conv = (convb_ref[:, cs] + convw_ref[0:1, cs] * y2 + convw_ref[1:2, cs] * y1
                + convw_ref[2:3, cs] * y)
        ya = (b_ * conv * _silu(z)).astype(bf16)
        acc_a = acc_a + _dot(ya, wbr_ref[0, cs, :])
    merged = jax.nn.sigmoid(_dot(h, wg_ref[:, 0:D_MODEL])) * acc_a

    merged = merged + jax.nn.sigmoid(_dot(h, wg_ref[:, D_MODEL:2 * D_MODEL])) * _dot(yb_ref[0], wbr_ref[1])

    v = jax.nn.gelu(_dot(h, wc_ref[:, MIX_W:2 * MIX_W]))
    mu = jnp.mean(v, axis=-1, keepdims=True)
    vc = v - mu
    vn = (vc * lax.rsqrt(jnp.mean(vc * vc, axis=-1, keepdims=True) + EPS)) * lng_ref[...] + lnb_ref[...]
    vn = vn.astype(bf16)
    ri = lax.broadcasted_iota(jnp.int32, (CHUNK, CHUNK), 0)
    ci = lax.broadcasted_iota(jnp.int32, (CHUNK, CHUNK), 1)
    acc_c = jnp.zeros((tm, D_MODEL), f32)
    gw = MIX_W // GM_GROUPS
    for gi in range(GM_GROUPS):
        cs = slice(gi * gw, (gi + 1) * gw)
        wm = jnp.where(ri >= ci, ws_ref[gi], 0.0).astype(bf16)
        sp = jnp.concatenate([_dot(wm, vn[k * CHUNK:(k + 1) * CHUNK, cs]) for k in range(tm // CHUNK)], axis=0)
        sp = sp + jnp.concatenate([bs_ref[:, cs]] * (tm // CHUNK), axis=0)
        u = jax.nn.gelu(_dot(h, wc_ref[:, cs]))
        z = _dot(h, wc_ref[:, 2 * MIX_W + gi * gw:2 * MIX_W + (gi + 1) * gw])
        yc = (u * sp * _silu(z)).astype(bf16)
        acc_c = acc_c + _dot(yc, wbr_ref[2, cs, :])
    merged = merged + jax.nn.sigmoid(_dot(h, wg_ref[:, 2 * D_MODEL:3 * D_MODEL])) * acc_c

    o = _dot(merged.astype(bf16), wout_ref[...])
    o = o * lax.rsqrt(jnp.mean(o * o, axis=-1, keepdims=True) + EPS) * gpost_ref[...]
    o_ref[0] = x + gate * o


def _post(x, yb, mod, g_pre, g_post, wa, wc, wg, wbr, wout, conv_w, conv_b, ln_g, ln_b, w_s, bs_full, tm):
    bsz, s, d = x.shape
    halo = 8

    def const(a):
        return pl.BlockSpec(a.shape, lambda b, i: (0,) * a.ndim, pipeline_mode=pl.Buffered(1))

    consts = (g_pre, g_post, wa, wc, wg, wbr, wout, conv_w, conv_b, ln_g, ln_b, w_s, bs_full)
    return pl.pallas_call(
        functools.partial(_post_kernel, tm=tm, cw=256),
        grid=(bsz, s // tm),
        in_specs=[pl.BlockSpec((1, tm, d), lambda b, i: (b, i, 0)),
                  pl.BlockSpec((1, halo, d), lambda b, i: (b, jnp.maximum(i * (tm // halo) - 1, 0), 0)),
                  pl.BlockSpec((1, tm, MIX_W), lambda b, i: (b, i, 0)),
                  pl.BlockSpec((1, 1, 3 * d), lambda b, i: (b, 0, 0))] + [const(a) for a in consts],
        out_specs=pl.BlockSpec((1, tm, d), lambda b, i: (b, i, 0)),
        out_shape=jax.ShapeDtypeStruct((bsz, s, d), f32),
        compiler_params=pltpu.CompilerParams(vmem_limit_bytes=VMEM_LIMIT),
        name="post",
    )(x, x, yb, mod, *consts)


def _pack_pairs(a, b):
    d = a.shape[0]
    return jnp.concatenate([a.reshape(d, N_KV, HEAD_DIM), b.reshape(d, N_KV, HEAD_DIM)], axis=-1).reshape(d, -1)


def _attn_weight(w_in_b):
    d = w_in_b.shape[0]
    sizes = [MIX_W] + [KV_W] * 6 + [MIX_W, 3 * N_HEADS]
    offs = [0]
    for sz in sizes:
        offs.append(offs[-1] + sz)
    q, kc, vc, ks, vs, kw, vw, z, gl = [w_in_b[:, offs[k]:offs[k + 1]] for k in range(len(sizes))]
    gl = jnp.pad(gl.reshape(d, N_KV, HPG * 3), ((0, 0), (0, 0), (0, LANES - HPG * 3))).reshape(d, N_KV * LANES)
    return jnp.concatenate([q, kc, vc, _pack_pairs(ks, vs), _pack_pairs(kw, vw), z, gl], axis=1).astype(bf16)


def _block_rows(t, s):
    bsz = t.shape[0]
    t = t.reshape(bsz, s // STRIDE_CMP, STRIDE_CMP, N_KV, HEAD_DIM).transpose(0, 3, 1, 2, 4)
    return t.reshape(bsz, N_KV, s // STRIDE_CMP, STRIDE_CMP * HEAD_DIM)


def _layer(x, mod, slopes, g_pre, g_post, w_in, conv_w, conv_b, pos_ck, w_ck1, w_ck2, pos_cv, w_cv1, w_cv2,
           ln_g, ln_b, w_s, b_s, w_br, w_out):
    bsz, s, d = x.shape
    assert s % KEY_CHUNK == 0 and s // L_SEL <= HEAD_DIM and s % 512 == 0
    row = lambda a: a.reshape(1, -1)
    pb = _proj(x, mod, row(g_pre), _attn_weight(w_in[:, B_OFF:B_OFF + B_COLS]), tm=512)

    zpad = jnp.zeros((CMP_HIDDEN, HEAD_DIM), f32)
    kvc = _compress(
        _block_rows(pb[:, :, KC_COL:KC_COL + KV_W], s), _block_rows(pb[:, :, VC_COL:VC_COL + KV_W], s),
        pos_ck.reshape(2, -1), pos_cv.reshape(2, -1), w_ck1.astype(bf16), w_cv1.astype(bf16),
        jnp.concatenate([w_ck2, zpad], axis=1).astype(bf16), jnp.concatenate([zpad, w_cv2], axis=1).astype(bf16))
    yb = _attention(slopes, pb, kvc, tq=Q_TILE)

    bs_full = jnp.repeat(b_s.T, MIX_W // GM_GROUPS, axis=1)
    return _post(x, yb, mod, row(g_pre), row(g_post),
                 w_in[:, A_OFF:A_OFF + A_COLS].astype(bf16), w_in[:, C_OFF:C_OFF + C_COLS].astype(bf16),
                 w_in[:, G_OFF:G_OFF + N_BRANCH * D_MODEL].astype(bf16), w_br.astype(bf16), w_out.astype(bf16),
                 conv_w, row(conv_b), row(ln_g), row(ln_b), w_s, bs_full, tm=256)


def kernel(x, c, g_pre, g_post, w_ada, b_ada, w_in, conv_w, conv_b, pos_ck, w_ck1, w_ck2, pos_cv, w_cv1, w_cv2,
           ln_g, ln_b, w_s, b_s, w_br, w_out):
    depth = w_in.shape[0]
    bsz = x.shape[0]
    mods = _ada(c, w_ada, b_ada).reshape(depth, bsz, 1, 3 * D_MODEL)
    head = jnp.arange(1, N_HEADS + 1, dtype=f32)
    slopes = (2.0 ** (-8.0 * head / N_HEADS)).reshape(N_KV, HPG)
    for l in range(depth):
        x = _layer(x, mods[l], slopes, g_pre[l], g_post[l], w_in[l], conv_w[l], conv_b[l], pos_ck[l], w_ck1[l],
                   w_ck2[l], pos_cv[l], w_cv1[l], w_cv2[l], ln_g[l], ln_b[l], w_s[l], b_s[l], w_br[l], w_out[l])
    return x
```

```python
import functools

import jax
import jax.numpy as jnp
from jax import lax
from jax.experimental import pallas as pl
from jax.experimental.pallas import tpu as pltpu

D_MODEL = 1024
MIX_W = 1024
CONV_K = 3
N_HEADS = 16
HEAD_DIM = 64
N_KV = 4
HPG = N_HEADS // N_KV
KV_W = N_KV * HEAD_DIM
L_CMP = 32
STRIDE_CMP = 16
CMP_HIDDEN = 128
L_SEL = 64
N_SEL_BLOCKS = 16
WINDOW = 512
CHUNK = 128
GM_GROUPS = 8
N_BRANCH = 3
EPS = 1e-6
NEG = -1e30

A_OFF = 0
A_COLS = 4 * MIX_W
B_OFF = A_OFF + A_COLS
B_COLS = 2 * MIX_W + 6 * KV_W + 3 * N_HEADS
C_OFF = B_OFF + B_COLS
C_COLS = 3 * MIX_W
G_OFF = C_OFF + C_COLS

LANES = 128
PAIR_W = 2 * HEAD_DIM
Q_COL = 0
KC_COL = Q_COL + MIX_W
VC_COL = KC_COL + KV_W
KVS_COL = VC_COL + KV_W
KVW_COL = KVS_COL + N_KV * PAIR_W
Z_COL = KVW_COL + N_KV * PAIR_W
GL_COL = Z_COL + MIX_W
PB_COLS = GL_COL + N_KV * LANES

KEY_CHUNK = 512
Q_TILE = 256
MASK_BIG = 2.0 ** 100
LOG2E = 1.4426950408889634
VMEM_LIMIT = 60 * 1024 * 1024

f32 = jnp.float32
bf16 = jnp.bfloat16


def _dot(a, b):
    return jnp.dot(a, b, preferred_element_type=f32)


def _silu(x):
    return x * jax.nn.sigmoid(x)


def _norm_mod(x, g, scl, shift):
    y = x * lax.rsqrt(jnp.mean(x * x, axis=-1, keepdims=True) + EPS)
    return (y * g) * (1.0 + scl) + shift


def _ada_kernel(c_ref, w_ref, b_ref, o_ref):
    c = c_ref[...]
    o_ref[0] = jnp.dot(_silu(c), w_ref[0], preferred_element_type=f32,
                       precision=lax.Precision.HIGHEST) + b_ref[0]


def _ada(c, w_ada, b_ada):
    depth, d, n = w_ada.shape
    bsz = c.shape[0]
    return pl.pallas_call(
        _ada_kernel,
        grid=(depth,),
        in_specs=[pl.BlockSpec((bsz, d), lambda l: (0, 0)),
                  pl.BlockSpec((1, d, n), lambda l: (l, 0, 0)),
                  pl.BlockSpec((1, 1, n), lambda l: (l, 0, 0))],
        out_specs=pl.BlockSpec((1, bsz, n), lambda l: (l, 0, 0)),
        out_shape=jax.ShapeDtypeStruct((depth, bsz, n), f32),
        compiler_params=pltpu.CompilerParams(vmem_limit_bytes=VMEM_LIMIT),
        name="ada",
    )(c, w_ada, b_ada.reshape(depth, 1, n))


def _proj_kernel(x_ref, mod_ref, g_ref, w_ref, o_ref, *, col_step):
    mod = mod_ref[0]
    h = _norm_mod(x_ref[0], g_ref[...], mod[:, D_MODEL:2 * D_MODEL], mod[:, :D_MODEL]).astype(bf16)
    for c0 in range(0, PB_COLS, col_step):
        o_ref[0, :, c0:c0 + col_step] = _dot(h, w_ref[:, c0:c0 + col_step]).astype(bf16)


def _proj(x, mod, g_pre, wb, tm):
    bsz, s, d = x.shape
    return pl.pallas_call(
        functools.partial(_proj_kernel, col_step=512),
        grid=(bsz, s // tm),
        in_specs=[pl.BlockSpec((1, tm, d), lambda b, i: (b, i, 0)),
                  pl.BlockSpec((1, 1, 3 * d), lambda b, i: (b, 0, 0)),
                  pl.BlockSpec((1, d), lambda b, i: (0, 0)),
                  pl.BlockSpec((d, PB_COLS), lambda b, i: (0, 0))],
        out_specs=pl.BlockSpec((1, tm, PB_COLS), lambda b, i: (b, i, 0)),
        out_shape=jax.ShapeDtypeStruct((bsz, s, PB_COLS), bf16),
        compiler_params=pltpu.CompilerParams(vmem_limit_bytes=VMEM_LIMIT),
        name="proj",
    )(x, mod, g_pre, wb)


def _cmp_kernel(tk_ref, tv_ref, pk_ref, pv_ref, w1k_ref, w1v_ref, w2k_ref, w2v_ref, o_ref):
    half = STRIDE_CMP * HEAD_DIM

    def hidden(t_ref, p_ref, w1_ref):
        t = t_ref[0, 0].astype(f32)
        n_rows = t.shape[0]
        ta = (t + p_ref[0:1, :]).astype(bf16)
        tb = (t + p_ref[1:2, :]).astype(bf16)
        a = _dot(ta, w1_ref[0:half, :])
        b = _dot(tb, w1_ref[half:2 * half, :])
        return _silu(a + pltpu.roll(b, n_rows - 1, axis=0)).astype(bf16)

    hk = hidden(tk_ref, pk_ref, w1k_ref)
    hv = hidden(tv_ref, pv_ref, w1v_ref)
    o_ref[0, 0] = (_dot(hk, w2k_ref[...]) + _dot(hv, w2v_ref[...])).astype(bf16)


def _compress(tk, tv, posk, posv, w1k, w1v, w2k, w2v):
    bsz, g, n_rows, width = tk.shape
    tspec = pl.BlockSpec((1, 1, n_rows, width), lambda b, j: (b, j, 0, 0))

    def full(a):
        return pl.BlockSpec(a.shape, lambda b, j: (0,) * a.ndim)

    return pl.pallas_call(
        _cmp_kernel,
        grid=(bsz, g),
        in_specs=[tspec, tspec, full(posk), full(posv), full(w1k), full(w1v), full(w2k), full(w2v)],
        out_specs=pl.BlockSpec((1, 1, n_rows, PAIR_W), lambda b, j: (b, j, 0, 0)),
        out_shape=jax.ShapeDtypeStruct((bsz, g, n_rows, PAIR_W), bf16),
        compiler_params=pltpu.CompilerParams(vmem_limit_bytes=VMEM_LIMIT),
        name="compress",
    )(tk, tv, posk, posv, w1k, w1v, w2k, w2v)


def _attn_kernel(slopes_ref, q_ref, kvs_ref, kvst_ref, kvw_ref, kvwt_ref, kvc_ref, kvct_ref, kfeat_ref, cfeat_ref,
                 z_ref, gl_ref, o_ref, *, tq, n_cmp_pad):
    g = pl.program_id(1)
    qt = pl.program_id(2)
    t0 = qt * tq
    cols = HPG * tq
    kc = KEY_CHUNK
    c_hi = t0 // kc

    qf = q_ref[0].astype(f32) * (HEAD_DIM ** -0.5 * LOG2E)
    qts = []
    for m in range(HPG // 2):
        t = qf[:, m * LANES:(m + 1) * LANES].T
        qts += [t[:HEAD_DIM], t[HEAD_DIM:]]
    q_t = jnp.concatenate(qts, axis=1)

    lane = lax.broadcasted_iota(jnp.int32, (1, cols), 1)
    head = lane // tq
    trow = lane - head * tq
    slope = LOG2E * jnp.where(head == 0, slopes_ref[g, 0],
                              jnp.where(head == 1, slopes_ref[g, 1],
                                        jnp.where(head == 2, slopes_ref[g, 2], slopes_ref[g, 3])))
    s_hi = slope.astype(bf16).astype(f32)
    s_mid = (slope - s_hi).astype(bf16).astype(f32)
    s_lo = slope - s_hi - s_mid
    r8 = lax.broadcasted_iota(jnp.int32, (8, cols), 0)
    blk = float(L_SEL)
    feat = jnp.where(r8 == 0, blk * s_hi, jnp.where(r8 == 1, blk * s_mid, jnp.where(r8 == 2, blk * s_lo,
           jnp.where(r8 == 3, s_hi, jnp.where(r8 == 4, s_mid, jnp.where(r8 == 5, s_lo, 0.0))))))
    zeros_h = jnp.zeros((HEAD_DIM, cols), f32)
    top = jnp.concatenate([q_t, zeros_h, feat, jnp.zeros((HEAD_DIM - 8, cols), f32)], axis=0)
    qw_t = jnp.concatenate([top, zeros_h], axis=0).astype(bf16)

    kvc = kvc_ref[0, 0]
    s = _dot(jnp.concatenate([kvc, cfeat_ref[...]], axis=1), qw_t)
    last_c = (t0 + trow - (L_CMP - 1)) // STRIDE_CMP
    s = jnp.where(lax.broadcasted_iota(jnp.int32, (n_cmp_pad, cols), 0) <= last_c, s, NEG)
    e = jnp.exp2(s - jnp.max(s, axis=0, keepdims=True))
    l_c = jnp.sum(e, axis=0, keepdims=True)
    p_c = e * jnp.where(last_c >= 0, 1.0 / l_c, 0.0)
    o_cmp = _dot(kvct_ref[0, 0], p_c.astype(bf16))

    assert kc <= WINDOW
    key_r = lax.broadcasted_iota(jnp.int32, (kc, cols), 0)

    def causal(c, s_chunk):
        return jnp.where(key_r <= trow + (t0 - c * kc), s_chunk, NEG)

    def load(c, ref, reft):
        k0 = pl.multiple_of(c * kc, kc)
        k_aug = jnp.concatenate([ref[0, pl.ds(k0, kc), :], kfeat_ref[pl.ds(k0, kc), :]], axis=1)
        return k_aug, reft[0, :, pl.ds(k0, kc)]

    def softmax_step(carry, s_chunk, kvt):
        m_prev, l_prev, acc = carry
        m_new = jnp.maximum(m_prev, jnp.max(s_chunk, axis=0, keepdims=True))
        alpha = jnp.exp2(m_prev - m_new)
        p = jnp.exp2(s_chunk - m_new)
        l_new = alpha * l_prev + jnp.sum(p, axis=0, keepdims=True)
        return m_new, l_new, alpha * acc + _dot(kvt, p.astype(bf16))

    def first_step(s_chunk, kvt):
        m = jnp.max(s_chunk, axis=0, keepdims=True)
        p = jnp.exp2(s_chunk - m)
        return m, jnp.sum(p, axis=0, keepdims=True), _dot(kvt, p.astype(bf16))

    def finish(carry):
        _, l, acc = carry
        return acc * (1.0 / l)

    k_aug, kvt = load(c_hi, kvw_ref, kvwt_ref)
    win_carry = first_step(causal(c_hi, _dot(k_aug, qw_t)), kvt)

    p_sum = p_c[:, 0:tq] + p_c[:, tq:2 * tq] + p_c[:, 2 * tq:3 * tq] + p_c[:, 3 * tq:4 * tq]
    p_hi = p_sum.astype(bf16)
    r1 = p_sum - p_hi.astype(f32)
    p_mid = r1.astype(bf16)
    p_lo = (r1 - p_mid.astype(f32)).astype(bf16)
    n_blk = HEAD_DIM
    blk_r = lax.broadcasted_iota(jnp.int32, (n_blk, n_cmp_pad), 0)
    cmp_c = lax.broadcasted_iota(jnp.int32, (n_blk, n_cmp_pad), 1)
    ratio = L_SEL // STRIDE_CMP
    ovl = jnp.where((cmp_c >= ratio * blk_r - (L_CMP // STRIDE_CMP - 1)) & (cmp_c <= ratio * blk_r + ratio - 1),
                    1.0, 0.0).astype(bf16)
    imp = jnp.maximum(_dot(ovl, p_hi) + _dot(ovl, p_mid) + _dot(ovl, p_lo), 0.0)
    srow = lax.broadcasted_iota(jnp.int32, (n_blk, tq), 0)
    cur = (t0 + lax.broadcasted_iota(jnp.int32, (n_blk, tq), 1)) // L_SEL
    forced = (srow == 0) | (srow == cur) | (srow == cur - 1)
    work = jnp.where((srow >= 1) & (srow <= cur - 2), imp, -1.0)
    sel = jnp.where(forced, 1.0, 0.0)
    srow_f = srow.astype(f32)
    for _ in range(N_SEL_BLOCKS - 3):
        best = jnp.max(work, axis=0, keepdims=True)
        first = jnp.min(jnp.where(work == best, srow_f, float(n_blk)), axis=0, keepdims=True)
        pick = (srow_f == first) & (best >= 0.0)
        sel = jnp.where(pick, 1.0, sel)
        work = jnp.where(pick, -1.0, work)
    pen = (sel - 1.0) * MASK_BIG
    qs_t = jnp.concatenate([top, jnp.concatenate([pen] * HPG, axis=1)], axis=0).astype(bf16)

    for back in range(1, (WINDOW + kc - 1) // kc + 1):
        c = c_hi - back
        k_aug, kvt = load(jnp.maximum(c, 0), kvw_ref, kvwt_ref)
        low = jnp.where(c >= 0, trow + (t0 - c * kc - WINDOW), kc)
        win_carry = softmax_step(win_carry, jnp.where(key_r > low, _dot(k_aug, qw_t), NEG), kvt)
    o_win = finish(win_carry)

    k_aug, kvt = load(c_hi, kvs_ref, kvst_ref)
    carry = first_step(causal(c_hi, _dot(k_aug, qs_t)), kvt)

    def slc_step(c, carry):
        k_aug, kvt = load(c, kvs_ref, kvst_ref)
        return softmax_step(carry, _dot(k_aug, qs_t), kvt)

    def slc_pair(i, streams):
        return slc_step(2 * i, streams[0]), slc_step(2 * i + 1, streams[1])

    empty = (jnp.full((1, cols), NEG, f32), jnp.zeros((1, cols), f32), jnp.zeros((HEAD_DIM, cols), f32))
    st_a, st_b = lax.fori_loop(0, c_hi // 2, slc_pair, (carry, empty))
    st_a = lax.fori_loop(0, c_hi % 2, lambda _, st: slc_step(c_hi - 1, st), st_a)
    m_ab = jnp.maximum(st_a[0], st_b[0])
    w_a, w_b = jnp.exp2(st_a[0] - m_ab), jnp.exp2(st_b[0] - m_ab)
    o_slc = finish((m_ab, st_a[1] * w_a + st_b[1] * w_b, st_a[2] * w_a + st_b[2] * w_b))

    gates_t = jax.nn.sigmoid(gl_ref[0].astype(f32)).T
    outs = []
    for n in range(HPG):
        cs = slice(n * tq, (n + 1) * tq)
        outs.append(gates_t[3 * n:3 * n + 1] * o_cmp[:, cs] + gates_t[3 * n + 1:3 * n + 2] * o_slc[:, cs]
                    + gates_t[3 * n + 2:3 * n + 3] * o_win[:, cs])
    slabs = [jnp.concatenate(outs[2 * m:2 * m + 2], axis=0).T for m in range(HPG // 2)]
    o = jnp.concatenate(slabs, axis=1)
    o_ref[0] = (o * _silu(z_ref[0].astype(f32))).astype(bf16)


def _key_features(pos):
    hi, lo = (pos // L_SEL).astype(f32), (pos % L_SEL).astype(f32)
    lanes = jnp.arange(LANES)
    feat = jnp.where(lanes[None, :] < 3, hi[:, None], jnp.where(lanes[None, :] < 6, lo[:, None], 0.0))
    onehot = (lanes[None, :] - HEAD_DIM == (pos // L_SEL)[:, None]).astype(f32)
    return (feat + onehot).astype(bf16)


def _attention(slopes, pb, kvc, tq):
    bsz, s, _ = pb.shape
    n_cmp_pad = kvc.shape[2]
    grp_w = HPG * HEAD_DIM
    pair_cols = N_KV * PAIR_W
    def values_t(col):
        pairs = pb[:, :, col:col + pair_cols].reshape(bsz, s, N_KV, 2, HEAD_DIM)
        return pairs[:, :, :, 1].reshape(bsz, s, KV_W).transpose(0, 2, 1)

    kvst, kvwt = values_t(KVS_COL), values_t(KVW_COL)
    kvct = kvc[:, :, :, HEAD_DIM:].transpose(0, 1, 3, 2)
    kfeat = _key_features(jnp.arange(s))
    cpos = jnp.arange(n_cmp_pad) * STRIDE_CMP + (L_CMP - 1)
    cfeat = jnp.where(jnp.arange(LANES)[None, :] < HEAD_DIM, _key_features(cpos), 0).astype(bf16)
    return pl.pallas_call(
        functools.partial(_attn_kernel, tq=tq, n_cmp_pad=n_cmp_pad),
        grid=(bsz, N_KV, s // tq),
        in_specs=[pl.BlockSpec(memory_space=pltpu.SMEM),
                  pl.BlockSpec((1, tq, grp_w), lambda b, g, i: (b, i, Q_COL // grp_w + g)),
                  pl.BlockSpec((1, s, PAIR_W), lambda b, g, i: (b, 0, KVS_COL // PAIR_W + g)),
                  pl.BlockSpec((1, HEAD_DIM, s), lambda b, g, i: (b, g, 0)),
                  pl.BlockSpec((1, s, PAIR_W), lambda b, g, i: (b, 0, KVW_COL // PAIR_W + g)),
                  pl.BlockSpec((1, HEAD_DIM, s), lambda b, g, i: (b, g, 0)),
                  pl.BlockSpec((1, 1, n_cmp_pad, PAIR_W), lambda b, g, i: (b, g, 0, 0)),
                  pl.BlockSpec((1, 1, HEAD_DIM, n_cmp_pad), lambda b, g, i: (b, g, 0, 0)),
                  pl.BlockSpec((s, LANES), lambda b, g, i: (0, 0)),
                  pl.BlockSpec((n_cmp_pad, LANES), lambda b, g, i: (0, 0)),
                  pl.BlockSpec((1, tq, grp_w), lambda b, g, i: (b, i, Z_COL // grp_w + g)),
                  pl.BlockSpec((1, tq, LANES), lambda b, g, i: (b, i, GL_COL // LANES + g))],
        out_specs=pl.BlockSpec((1, tq, grp_w), lambda b, g, i: (b, i, g)),
        out_shape=jax.ShapeDtypeStruct((bsz, s, MIX_W), bf16),
        compiler_params=pltpu.CompilerParams(vmem_limit_bytes=VMEM_LIMIT),
        name="attn",
    )(slopes, pb, pb, kvst, pb, kvwt, kvc, kvct, kfeat, cfeat, pb, pb)


def _post_kernel(x_ref, xh_ref, yb_ref, mod_ref, gpre_ref, gpost_ref, wa_ref, wc_ref, wg_ref, wbr_ref, wout_ref,
                 convw_ref, convb_ref, lng_ref, lnb_ref, ws_ref, bs_ref, o_ref, *, tm, cw):
    i = pl.program_id(1)
    mod = mod_ref[0]
    shift, scl, gate = mod[:, :D_MODEL], mod[:, D_MODEL:2 * D_MODEL], mod[:, 2 * D_MODEL:]
    x = x_ref[0]
    h = _norm_mod(x, gpre_ref[...], scl, shift).astype(bf16)
    hh = _norm_mod(xh_ref[0], gpre_ref[...], scl, shift).astype(bf16)
    row = lax.broadcasted_iota(jnp.int32, (tm, cw), 0)
    n_cc = MIX_W // cw

    acc_a = jnp.zeros((tm, D_MODEL), f32)
    for j in range(n_cc):
        def col(k):
            return slice(k * MIX_W + j * cw, k * MIX_W + (j + 1) * cw)
        b_ = _dot(h, wa_ref[:, col(0)])
        y = _dot(h, wa_ref[:, col(1)]) * _dot(h, wa_ref[:, col(2)])
        z = _dot(h, wa_ref[:, col(3)])
        yh = _dot(hh, wa_ref[:, col(1)]) * _dot(hh, wa_ref[:, col(2)])
        yh = jnp.where(i > 0, yh, 0.0)
        y1 = jnp.where(row == 0, yh[7:8, :], pltpu.roll(y, 1, axis=0))
        y2 = jnp.where(row == 0, yh[6:7, :], jnp.where(row == 1, yh[7:8, :], pltpu.roll(y, 2, axis=0)))
        cs = slice(j * cw, (j + 1) * cw)
        conv = (convb_ref[:, cs] + convw_ref[0:1, cs] * y2 + convw_ref[1:2, cs] * y1
                + convw_ref[2:3, cs] * y)
        ya = (b_ * conv * _silu(z)).astype(bf16)
        acc_a = acc_a + _dot(ya, wbr_ref[0, cs, :])
    merged = jax.nn.sigmoid(_dot(h, wg_ref[:, 0:D_MODEL])) * acc_a

    merged = merged + jax.nn.sigmoid(_dot(h, wg_ref[:, D_MODEL:2 * D_MODEL])) * _dot(yb_ref[0], wbr_ref[1])

    v = jax.nn.gelu(_dot(h, wc_ref[:, MIX_W:2 * MIX_W]))
    mu = jnp.mean(v, axis=-1, keepdims=True)
    vc = v - mu
    vn = (vc * lax.rsqrt(jnp.mean(vc * vc, axis=-1, keepdims=True) + EPS)) * lng_ref[...] + lnb_ref[...]
    vn = vn.astype(bf16)
    ri = lax.broadcasted_iota(jnp.int32, (CHUNK, CHUNK), 0)
    ci = lax.broadcasted_iota(jnp.int32, (CHUNK, CHUNK), 1)
    acc_c = jnp.zeros((tm, D_MODEL), f32)
    gw = MIX_W // GM_GROUPS
    for gi in range(GM_GROUPS):
        cs = slice(gi * gw, (gi + 1) * gw)
        wm = jnp.where(ri >= ci, ws_ref[gi], 0.0).astype(bf16)
        sp = jnp.concatenate([_dot(wm, vn[k * CHUNK:(k + 1) * CHUNK, cs]) for k in range(tm // CHUNK)], axis=0)
        sp = sp + jnp.concatenate([bs_ref[:, cs]] * (tm // CHUNK), axis=0)
        u = jax.nn.gelu(_dot(h, wc_ref[:, cs]))
        z = _dot(h, wc_ref[:, 2 * MIX_W + gi * gw:2 * MIX_W + (gi + 1) * gw])
        yc = (u * sp * _silu(z)).astype(bf16)
        acc_c = acc_c + _dot(yc, wbr_ref[2, cs, :])
    merged = merged + jax.nn.sigmoid(_dot(h, wg_ref[:, 2 * D_MODEL:3 * D_MODEL])) * acc_c

    o = _dot(merged.astype(bf16), wout_ref[...])
    o = o * lax.rsqrt(jnp.mean(o * o, axis=-1, keepdims=True) + EPS) * gpost_ref[...]
    o_ref[0] = x + gate * o


def _post(x, yb, mod, g_pre, g_post, wa, wc, wg, wbr, wout, conv_w, conv_b, ln_g, ln_b, w_s, bs_full, tm):
    bsz, s, d = x.shape
    halo = 8

    def const(a):
        return pl.BlockSpec(a.shape, lambda b, i: (0,) * a.ndim, pipeline_mode=pl.Buffered(1))

    consts = (g_pre, g_post, wa, wc, wg, wbr, wout, conv_w, conv_b, ln_g, ln_b, w_s, bs_full)
    return pl.pallas_call(
        functools.partial(_post_kernel, tm=tm, cw=256),
        grid=(bsz, s // tm),
        in_specs=[pl.BlockSpec((1, tm, d), lambda b, i: (b, i, 0)),
                  pl.BlockSpec((1, halo, d), lambda b, i: (b, jnp.maximum(i * (tm // halo) - 1, 0), 0)),
                  pl.BlockSpec((1, tm, MIX_W), lambda b, i: (b, i, 0)),
                  pl.BlockSpec((1, 1, 3 * d), lambda b, i: (b, 0, 0))] + [const(a) for a in consts],
        out_specs=pl.BlockSpec((1, tm, d), lambda b, i: (b, i, 0)),
        out_shape=jax.ShapeDtypeStruct((bsz, s, d), f32),
        compiler_params=pltpu.CompilerParams(vmem_limit_bytes=VMEM_LIMIT),
        name="post",
    )(x, x, yb, mod, *consts)


def _pack_pairs(a, b):
    d = a.shape[0]
    return jnp.concatenate([a.reshape(d, N_KV, HEAD_DIM), b.reshape(d, N_KV, HEAD_DIM)], axis=-1).reshape(d, -1)


def _attn_weight(w_in_b):
    d = w_in_b.shape[0]
    sizes = [MIX_W] + [KV_W] * 6 + [MIX_W, 3 * N_HEADS]
    offs = [0]
    for sz in sizes:
        offs.append(offs[-1] + sz)
    q, kc, vc, ks, vs, kw, vw, z, gl = [w_in_b[:, offs[k]:offs[k + 1]] for k in range(len(sizes))]
    gl = jnp.pad(gl.reshape(d, N_KV, HPG * 3), ((0, 0), (0, 0), (0, LANES - HPG * 3))).reshape(d, N_KV * LANES)
    return jnp.concatenate([q, kc, vc, _pack_pairs(ks, vs), _pack_pairs(kw, vw), z, gl], axis=1).astype(bf16)


def _block_rows(t, s):
    bsz = t.shape[0]
    t = t.reshape(bsz, s // STRIDE_CMP, STRIDE_CMP, N_KV, HEAD_DIM).transpose(0, 3, 1, 2, 4)
    return t.reshape(bsz, N_KV, s // STRIDE_CMP, STRIDE_CMP * HEAD_DIM)


def _layer(x, mod, slopes, g_pre, g_post, w_in, conv_w, conv_b, pos_ck, w_ck1, w_ck2, pos_cv, w_cv1, w_cv2,
           ln_g, ln_b, w_s, b_s, w_br, w_out):
    bsz, s, d = x.shape
    assert s % KEY_CHUNK == 0 and s // L_SEL <= HEAD_DIM and s % 512 == 0
    row = lambda a: a.reshape(1, -1)
    pb = _proj(x, mod, row(g_pre), _attn_weight(w_in[:, B_OFF:B_OFF + B_COLS]), tm=512)

    zpad = jnp.zeros((CMP_HIDDEN, HEAD_DIM), f32)
    kvc = _compress(
        _block_rows(pb[:, :, KC_COL:KC_COL + KV_W], s), _block_rows(pb[:, :, VC_COL:VC_COL + KV_W], s),
        pos_ck.reshape(2, -1), pos_cv.reshape(2, -1), w_ck1.astype(bf16), w_cv1.astype(bf16),
        jnp.concatenate([w_ck2, zpad], axis=1).astype(bf16), jnp.concatenate([zpad, w_cv2], axis=1).astype(bf16))
    yb = _attention(slopes, pb, kvc, tq=Q_TILE)

    bs_full = jnp.repeat(b_s.T, MIX_W // GM_GROUPS, axis=1)
    return _post(x, yb, mod, row(g_pre), row(g_post),
                 w_in[:, A_OFF:A_OFF + A_COLS].astype(bf16), w_in[:, C_OFF:C_OFF + C_COLS].astype(bf16),
                 w_in[:, G_OFF:G_OFF + N_BRANCH * D_MODEL].astype(bf16), w_br.astype(bf16), w_out.astype(bf16),
                 conv_w, row(conv_b), row(ln_g), row(ln_b), w_s, bs_full, tm=512)


def kernel(x, c, g_pre, g_post, w_ada, b_ada, w_in, conv_w, conv_b, pos_ck, w_ck1, w_ck2, pos_cv, w_cv1, w_cv2,
           ln_g, ln_b, w_s, b_s, w_br, w_out):
    depth = w_in.shape[0]
    bsz = x.shape[0]
    mods = _ada(c, w_ada, b_ada).reshape(depth, bsz, 1, 3 * D_MODEL)
    head = jnp.arange(1, N_HEADS + 1, dtype=f32)
    slopes = (2.0 ** (-8.0 * head / N_HEADS)).reshape(N_KV, HPG)
    for l in range(depth):
        x = _layer(x, mods[l], slopes, g_pre[l], g_post[l], w_in[l], conv_w[l], conv_b[l], pos_ck[l], w_ck1[l],
                   w_ck2[l], pos_cv[l], w_cv1[l], w_cv2[l], ln_g[l], ln_b[l], w_s[l], b_s[l], w_br[l], w_out[l])
    return x
```

```python
import functools

import jax
import jax.numpy as jnp
from jax import lax
from jax.experimental import pallas as pl
from jax.experimental.pallas import tpu as pltpu

D_MODEL = 1024
MIX_W = 1024
CONV_K = 3
N_HEADS = 16
HEAD_DIM = 64
N_KV = 4
HPG = N_HEADS // N_KV
KV_W = N_KV * HEAD_DIM
L_CMP = 32
STRIDE_CMP = 16
CMP_HIDDEN = 128
L_SEL = 64
N_SEL_BLOCKS = 16
WINDOW = 512
CHUNK = 128
GM_GROUPS = 8
N_BRANCH = 3
EPS = 1e-6
NEG = -1e30

A_OFF = 0
A_COLS = 4 * MIX_W
B_OFF = A_OFF + A_COLS
B_COLS = 2 * MIX_W + 6 * KV_W + 3 * N_HEADS
C_OFF = B_OFF + B_COLS
C_COLS = 3 * MIX_W
G_OFF = C_OFF + C_COLS

LANES = 128
PAIR_W = 2 * HEAD_DIM
Q_COL = 0
KC_COL = Q_COL + MIX_W
VC_COL = KC_COL + KV_W
KVS_COL = VC_COL + KV_W
KVW_COL = KVS_COL + N_KV * PAIR_W
Z_COL = KVW_COL + N_KV * PAIR_W
GL_COL = Z_COL + MIX_W
PB_COLS = GL_COL + N_KV * LANES

KEY_CHUNK = 512
Q_TILE = 512
MASK_BIG = 2.0 ** 100
LOG2E = 1.4426950408889634
ONES_ROWS = 16
VMEM_LIMIT = 60 * 1024 * 1024

f32 = jnp.float32
bf16 = jnp.bfloat16


def _dot(a, b):
    return jnp.dot(a, b, preferred_element_type=f32)


def _silu(x):
    return x * jax.nn.sigmoid(x)


def _norm_mod(x, g, scl, shift):
    y = x * lax.rsqrt(jnp.mean(x * x, axis=-1, keepdims=True) + EPS)
    return (y * g) * (1.0 + scl) + shift


def _ada_kernel(c_ref, w_ref, b_ref, o_ref):
    c = c_ref[...]
    o_ref[0] = jnp.dot(_silu(c), w_ref[0], preferred_element_type=f32,
                       precision=lax.Precision.HIGHEST) + b_ref[0]


def _ada(c, w_ada, b_ada):
    depth, d, n = w_ada.shape
    bsz = c.shape[0]
    return pl.pallas_call(
        _ada_kernel,
        grid=(depth,),
        in_specs=[pl.BlockSpec((bsz, d), lambda l: (0, 0)),
                  pl.BlockSpec((1, d, n), lambda l: (l, 0, 0)),
                  pl.BlockSpec((1, 1, n), lambda l: (l, 0, 0))],
        out_specs=pl.BlockSpec((1, bsz, n), lambda l: (l, 0, 0)),
        out_shape=jax.ShapeDtypeStruct((depth, bsz, n), f32),
        compiler_params=pltpu.CompilerParams(vmem_limit_bytes=VMEM_LIMIT),
        name="ada",
    )(c, w_ada, b_ada.reshape(depth, 1, n))


def _proj_kernel(x_ref, mod_ref, g_ref, w_ref, o_ref, *, col_step):
    mod = mod_ref[0]
    h = _norm_mod(x_ref[0], g_ref[...], mod[:, D_MODEL:2 * D_MODEL], mod[:, :D_MODEL]).astype(bf16)
    for c0 in range(0, PB_COLS, col_step):
        o_ref[0, :, c0:c0 + col_step] = _dot(h, w_ref[:, c0:c0 + col_step]).astype(bf16)


def _proj(x, mod, g_pre, wb, tm):
    bsz, s, d = x.shape
    return pl.pallas_call(
        functools.partial(_proj_kernel, col_step=512),
        grid=(bsz, s // tm),
        in_specs=[pl.BlockSpec((1, tm, d), lambda b, i: (b, i, 0)),
                  pl.BlockSpec((1, 1, 3 * d), lambda b, i: (b, 0, 0)),
                  pl.BlockSpec((1, d), lambda b, i: (0, 0)),
                  pl.BlockSpec((d, PB_COLS), lambda b, i: (0, 0))],
        out_specs=pl.BlockSpec((1, tm, PB_COLS), lambda b, i: (b, i, 0)),
        out_shape=jax.ShapeDtypeStruct((bsz, s, PB_COLS), bf16),
        compiler_params=pltpu.CompilerParams(vmem_limit_bytes=VMEM_LIMIT),
        name="proj",
    )(x, mod, g_pre, wb)


def _cmp_kernel(tk_ref, tv_ref, pk_ref, pv_ref, w1k_ref, w1v_ref, w2k_ref, w2v_ref, o_ref):
    half = STRIDE_CMP * HEAD_DIM

    def hidden(t_ref, p_ref, w1_ref):
        t = t_ref[0, 0].astype(f32)
        n_rows = t.shape[0]
        ta = (t + p_ref[0:1, :]).astype(bf16)
        tb = (t + p_ref[1:2, :]).astype(bf16)
        a = _dot(ta, w1_ref[0:half, :])
        b = _dot(tb, w1_ref[half:2 * half, :])
        return _silu(a + pltpu.roll(b, n_rows - 1, axis=0)).astype(bf16)

    hk = hidden(tk_ref, pk_ref, w1k_ref)
    hv = hidden(tv_ref, pv_ref, w1v_ref)
    o_ref[0, 0] = (_dot(hk, w2k_ref[...]) + _dot(hv, w2v_ref[...])).astype(bf16)


def _compress(tk, tv, posk, posv, w1k, w1v, w2k, w2v):
    bsz, g, n_rows, width = tk.shape
    tspec = pl.BlockSpec((1, 1, n_rows, width), lambda b, j: (b, j, 0, 0))

    def full(a):
        return pl.BlockSpec(a.shape, lambda b, j: (0,) * a.ndim)

    return pl.pallas_call(
        _cmp_kernel,
        grid=(bsz, g),
        in_specs=[tspec, tspec, full(posk), full(posv), full(w1k), full(w1v), full(w2k), full(w2v)],
        out_specs=pl.BlockSpec((1, 1, n_rows, PAIR_W), lambda b, j: (b, j, 0, 0)),
        out_shape=jax.ShapeDtypeStruct((bsz, g, n_rows, PAIR_W), bf16),
        compiler_params=pltpu.CompilerParams(vmem_limit_bytes=VMEM_LIMIT),
        name="compress",
    )(tk, tv, posk, posv, w1k, w1v, w2k, w2v)


def _attn_kernel(slopes_ref, q_ref, kvs_ref, kvst_ref, kvw_ref, kvwt_ref, kvc_ref, kvct_ref, kfeat_ref, cfeat_ref,
                 z_ref, gl_ref, o_ref, *, tq, n_cmp_pad):
    g = pl.program_id(1)
    qt = pl.program_id(2)
    t0 = qt * tq
    cols = HPG * tq
    kc = KEY_CHUNK
    c_hi = t0 // kc

    qf = q_ref[0].astype(f32) * (HEAD_DIM ** -0.5 * LOG2E)
    qts = []
    for m in range(HPG // 2):
        t = qf[:, m * LANES:(m + 1) * LANES].T
        qts += [t[:HEAD_DIM], t[HEAD_DIM:]]
    q_t = jnp.concatenate(qts, axis=1)

    lane = lax.broadcasted_iota(jnp.int32, (1, cols), 1)
    head = lane // tq
    trow = lane - head * tq
    slope = LOG2E * jnp.where(head == 0, slopes_ref[g, 0],
                              jnp.where(head == 1, slopes_ref[g, 1],
                                        jnp.where(head == 2, slopes_ref[g, 2], slopes_ref[g, 3])))
    s_hi = slope.astype(bf16).astype(f32)
    s_mid = (slope - s_hi).astype(bf16).astype(f32)
    s_lo = slope - s_hi - s_mid
    r8 = lax.broadcasted_iota(jnp.int32, (8, cols), 0)
    blk = float(L_SEL)
    feat = jnp.where(r8 == 0, blk * s_hi, jnp.where(r8 == 1, blk * s_mid, jnp.where(r8 == 2, blk * s_lo,
           jnp.where(r8 == 3, s_hi, jnp.where(r8 == 4, s_mid, jnp.where(r8 == 5, s_lo, 0.0))))))
    zeros_h = jnp.zeros((HEAD_DIM, cols), f32)
    top = jnp.concatenate([q_t, zeros_h, feat, jnp.zeros((HEAD_DIM - 8, cols), f32)], axis=0)
    qw_t = jnp.concatenate([top, zeros_h], axis=0).astype(bf16)

    kvc = kvc_ref[0, 0]
    s = _dot(jnp.concatenate([kvc, cfeat_ref[...]], axis=1), qw_t)
    last_c = (t0 + trow - (L_CMP - 1)) // STRIDE_CMP
    s = jnp.where(lax.broadcasted_iota(jnp.int32, (n_cmp_pad, cols), 0) <= last_c, s, NEG)
    e = jnp.exp2(s - jnp.max(s, axis=0, keepdims=True))
    l_c = jnp.sum(e, axis=0, keepdims=True)
    p_c = e * jnp.where(last_c >= 0, 1.0 / l_c, 0.0)
    o_cmp = _dot(kvct_ref[0, 0], p_c.astype(bf16))

    assert kc <= WINDOW
    key_r = lax.broadcasted_iota(jnp.int32, (kc, cols), 0)

    def causal(c, s_chunk):
        return jnp.where(key_r <= trow + (t0 - c * kc), s_chunk, NEG)

    def load(c, ref, reft):
        k0 = pl.multiple_of(c * kc, kc)
        k_aug = jnp.concatenate([ref[0, pl.ds(k0, kc), :], kfeat_ref[pl.ds(k0, kc), :]], axis=1)
        return k_aug, jnp.concatenate([reft[0, :, pl.ds(k0, kc)], jnp.ones((ONES_ROWS, kc), bf16)], axis=0)

    def softmax_step(carry, s_chunk, vt1):
        m_prev, acc = carry
        m_new = jnp.maximum(m_prev, jnp.max(s_chunk, axis=0, keepdims=True))
        p = jnp.exp2(s_chunk - m_new)
        return m_new, jnp.exp2(m_prev - m_new) * acc + _dot(vt1, p.astype(bf16))

    def first_step(s_chunk, vt1):
        m = jnp.max(s_chunk, axis=0, keepdims=True)
        return m, _dot(vt1, jnp.exp2(s_chunk - m).astype(bf16))

    def finish(carry):
        acc = carry[1]
        return acc[:HEAD_DIM] * (1.0 / acc[HEAD_DIM:HEAD_DIM + 1])

    k_aug, kvt = load(c_hi, kvw_ref, kvwt_ref)
    win_carry = first_step(causal(c_hi, _dot(k_aug, qw_t)), kvt)

    p_sum = p_c[:, 0:tq] + p_c[:, tq:2 * tq] + p_c[:, 2 * tq:3 * tq] + p_c[:, 3 * tq:4 * tq]
    p_hi = p_sum.astype(bf16)
    r1 = p_sum - p_hi.astype(f32)
    p_mid = r1.astype(bf16)
    p_lo = (r1 - p_mid.astype(f32)).astype(bf16)
    n_blk = HEAD_DIM
    blk_r = lax.broadcasted_iota(jnp.int32, (n_blk, n_cmp_pad), 0)
    cmp_c = lax.broadcasted_iota(jnp.int32, (n_blk, n_cmp_pad), 1)
    ratio = L_SEL // STRIDE_CMP
    ovl = jnp.where((cmp_c >= ratio * blk_r - (L_CMP // STRIDE_CMP - 1)) & (cmp_c <= ratio * blk_r + ratio - 1),
                    1.0, 0.0).astype(bf16)
    imp = jnp.maximum(_dot(ovl, p_hi) + _dot(ovl, p_mid) + _dot(ovl, p_lo), 0.0)
    srow = lax.broadcasted_iota(jnp.int32, (n_blk, tq), 0)
    cur = (t0 + lax.broadcasted_iota(jnp.int32, (n_blk, tq), 1)) // L_SEL
    forced = (srow == 0) | (srow == cur) | (srow == cur - 1)
    work = jnp.where((srow >= 1) & (srow <= cur - 2), imp, -1.0)
    sel = jnp.where(forced, 1.0, 0.0)
    srow_f = srow.astype(f32)
    for _ in range(N_SEL_BLOCKS - 3):
        best = jnp.max(work, axis=0, keepdims=True)
        first = jnp.min(jnp.where(work == best, srow_f, float(n_blk)), axis=0, keepdims=True)
        pick = (srow_f == first) & (best >= 0.0)
        sel = jnp.where(pick, 1.0, sel)
        work = jnp.where(pick, -1.0, work)
    pen = (sel - 1.0) * MASK_BIG
    qs_t = jnp.concatenate([top, jnp.concatenate([pen] * HPG, axis=1)], axis=0).astype(bf16)

    for back in range(1, (WINDOW + kc - 1) // kc + 1):
        c = c_hi - back
        k_aug, kvt = load(jnp.maximum(c, 0), kvw_ref, kvwt_ref)
        low = jnp.where(c >= 0, trow + (t0 - c * kc - WINDOW), kc)
        win_carry = softmax_step(win_carry, jnp.where(key_r > low, _dot(k_aug, qw_t), NEG), kvt)
    o_win = finish(win_carry)

    k_aug, kvt = load(c_hi, kvs_ref, kvst_ref)
    carry = first_step(causal(c_hi, _dot(k_aug, qs_t)), kvt)

    def slc_step(c, carry):
        k_aug, kvt = load(c, kvs_ref, kvst_ref)
        return softmax_step(carry, _dot(k_aug, qs_t), kvt)

    def slc_pair(i, streams):
        return slc_step(2 * i, streams[0]), slc_step(2 * i + 1, streams[1])

    empty = (jnp.full((1, cols), NEG, f32), jnp.zeros((HEAD_DIM + ONES_ROWS, cols), f32))
    st_a, st_b = lax.fori_loop(0, c_hi // 2, slc_pair, (carry, empty))
    st_a = lax.fori_loop(0, c_hi % 2, lambda _, st: slc_step(c_hi - 1, st), st_a)
    m_ab = jnp.maximum(st_a[0], st_b[0])
    o_slc = finish((m_ab, st_a[1] * jnp.exp2(st_a[0] - m_ab) + st_b[1] * jnp.exp2(st_b[0] - m_ab)))

    gates_t = jax.nn.sigmoid(gl_ref[0].astype(f32)).T
    outs = []
    for n in range(HPG):
        cs = slice(n * tq, (n + 1) * tq)
        outs.append(gates_t[3 * n:3 * n + 1] * o_cmp[:, cs] + gates_t[3 * n + 1:3 * n + 2] * o_slc[:, cs]
                    + gates_t[3 * n + 2:3 * n + 3] * o_win[:, cs])
    slabs = [jnp.concatenate(outs[2 * m:2 * m + 2], axis=0).T for m in range(HPG // 2)]
    o = jnp.concatenate(slabs, axis=1)
    o_ref[0] = (o * _silu(z_ref[0].astype(f32))).astype(bf16)


def _key_features(pos):
    hi, lo = (pos // L_SEL).astype(f32), (pos % L_SEL).astype(f32)
    lanes = jnp.arange(LANES)
    feat = jnp.where(lanes[None, :] < 3, hi[:, None], jnp.where(lanes[None, :] < 6, lo[:, None], 0.0))
    onehot = (lanes[None, :] - HEAD_DIM == (pos // L_SEL)[:, None]).astype(f32)
    return (feat + onehot).astype(bf16)


def _attention(slopes, pb, kvc, tq):
    bsz, s, _ = pb.shape
    n_cmp_pad = kvc.shape[2]
    grp_w = HPG * HEAD_DIM
    pair_cols = N_KV * PAIR_W
    def values_t(col):
        pairs = pb[:, :, col:col + pair_cols].reshape(bsz, s, N_KV, 2, HEAD_DIM)
        return pairs[:, :, :, 1].reshape(bsz, s, KV_W).transpose(0, 2, 1)

    kvst, kvwt = values_t(KVS_COL), values_t(KVW_COL)
    kvct = kvc[:, :, :, HEAD_DIM:].transpose(0, 1, 3, 2)
    kfeat = _key_features(jnp.arange(s))
    cpos = jnp.arange(n_cmp_pad) * STRIDE_CMP + (L_CMP - 1)
    cfeat = jnp.where(jnp.arange(LANES)[None, :] < HEAD_DIM, _key_features(cpos), 0).astype(bf16)
    return pl.pallas_call(
        functools.partial(_attn_kernel, tq=tq, n_cmp_pad=n_cmp_pad),
        grid=(bsz, N_KV, s // tq),
        in_specs=[pl.BlockSpec(memory_space=pltpu.SMEM),
                  pl.BlockSpec((1, tq, grp_w), lambda b, g, i: (b, i, Q_COL // grp_w + g)),
                  pl.BlockSpec((1, s, PAIR_W), lambda b, g, i: (b, 0, KVS_COL // PAIR_W + g)),
                  pl.BlockSpec((1, HEAD_DIM, s), lambda b, g, i: (b, g, 0)),
                  pl.BlockSpec((1, s, PAIR_W), lambda b, g, i: (b, 0, KVW_COL // PAIR_W + g)),
                  pl.BlockSpec((1, HEAD_DIM, s), lambda b, g, i: (b, g, 0)),
                  pl.BlockSpec((1, 1, n_cmp_pad, PAIR_W), lambda b, g, i: (b, g, 0, 0)),
                  pl.BlockSpec((1, 1, HEAD_DIM, n_cmp_pad), lambda b, g, i: (b, g, 0, 0)),
                  pl.BlockSpec((s, LANES), lambda b, g, i: (0, 0)),
                  pl.BlockSpec((n_cmp_pad, LANES), lambda b, g, i: (0, 0)),
                  pl.BlockSpec((1, tq, grp_w), lambda b, g, i: (b, i, Z_COL // grp_w + g)),
                  pl.BlockSpec((1, tq, LANES), lambda b, g, i: (b, i, GL_COL // LANES + g))],
        out_specs=pl.BlockSpec((1, tq, grp_w), lambda b, g, i: (b, i, g)),
        out_shape=jax.ShapeDtypeStruct((bsz, s, MIX_W), bf16),
        compiler_params=pltpu.CompilerParams(vmem_limit_bytes=VMEM_LIMIT),
        name="attn",
    )(slopes, pb, pb, kvst, pb, kvwt, kvc, kvct, kfeat, cfeat, pb, pb)


def _post_kernel(x_ref, xh_ref, yb_ref, mod_ref, gpre_ref, gpost_ref, wa_ref, wc_ref, wg_ref, wbr_ref, wout_ref,
                 convw_ref, convb_ref, lng_ref, lnb_ref, ws_ref, bs_ref, o_ref, *, tm, cw):
    i = pl.program_id(1)
    mod = mod_ref[0]
    shift, scl, gate = mod[:, :D_MODEL], mod[:, D_MODEL:2 * D_MODEL], mod[:, 2 * D_MODEL:]
    x = x_ref[0]
    h = _norm_mod(x, gpre_ref[...], scl, shift).astype(bf16)
    hh = _norm_mod(xh_ref[0], gpre_ref[...], scl, shift).astype(bf16)
    row = lax.broadcasted_iota(jnp.int32, (tm, cw), 0)
    n_cc = MIX_W // cw

    acc_a = jnp.zeros((tm, D_MODEL), f32)
    for j in range(n_cc):
        def col(k):
            return slice(k * MIX_W + j * cw, k * MIX_W + (j + 1) * cw)
        b_ = _dot(h, wa_ref[:, col(0)])
        y = _dot(h, wa_ref[:, col(1)]) * _dot(h, wa_ref[:, col(2)])
        z = _dot(h, wa_ref[:, col(3)])
        yh = _dot(hh, wa_ref[:, col(1)]) * _dot(hh, wa_ref[:, col(2)])
        yh = jnp.where(i > 0, yh, 0.0)
        y1 = jnp.where(row == 0, yh[7:8, :], pltpu.roll(y, 1, axis=0))
        y2 = jnp.where(row == 0, yh[6:7, :], jnp.where(row == 1, yh[7:8, :], pltpu.roll(y, 2, axis=0)))
        cs = slice(j * cw, (j + 1) * cw)
        conv = (convb_ref[:, cs] + convw_ref[0:1, cs] * y2 + convw_ref[1:2, cs] * y1
                + convw_ref[2:3, cs] * y)
        ya = (b_ * conv * _silu(z)).astype(bf16)
        acc_a = acc_a + _dot(ya, wbr_ref[0, cs, :])
    merged = jax.nn.sigmoid(_dot(h, wg_ref[:, 0:D_MODEL])) * acc_a

    merged = merged + jax.nn.sigmoid(_dot(h, wg_ref[:, D_MODEL:2 * D_MODEL])) * _dot(yb_ref[0], wbr_ref[1])

    v = jax.nn.gelu(_dot(h, wc_ref[:, MIX_W:2 * MIX_W]))
    mu = jnp.mean(v, axis=-1, keepdims=True)
    vc = v - mu
    vn = (vc * lax.rsqrt(jnp.mean(vc * vc, axis=-1, keepdims=True) + EPS)) * lng_ref[...] + lnb_ref[...]
    vn = vn.astype(bf16)
    ri = lax.broadcasted_iota(jnp.int32, (CHUNK, CHUNK), 0)
    ci = lax.broadcasted_iota(jnp.int32, (CHUNK, CHUNK), 1)
    acc_c = jnp.zeros((tm, D_MODEL), f32)
    gw = MIX_W // GM_GROUPS
    for gi in range(GM_GROUPS):
        cs = slice(gi * gw, (gi + 1) * gw)
        wm = jnp.where(ri >= ci, ws_ref[gi], 0.0).astype(bf16)
        sp = jnp.concatenate([_dot(wm, vn[k * CHUNK:(k + 1) * CHUNK, cs]) for k in range(tm // CHUNK)], axis=0)
        sp = sp + jnp.concatenate([bs_ref[:, cs]] * (tm // CHUNK), axis=0)
        u = jax.nn.gelu(_dot(h, wc_ref[:, cs]))
        z = _dot(h, wc_ref[:, 2 * MIX_W + gi * gw:2 * MIX_W + (gi + 1) * gw])
        yc = (u * sp * _silu(z)).astype(bf16)
        acc_c = acc_c + _dot(yc, wbr_ref[2, cs, :])
    merged = merged + jax.nn.sigmoid(_dot(h, wg_ref[:, 2 * D_MODEL:3 * D_MODEL])) * acc_c

    o = _dot(merged.astype(bf16), wout_ref[...])
    o = o * lax.rsqrt(jnp.mean(o * o, axis=-1, keepdims=True) + EPS) * gpost_ref[...]
    o_ref[0] = x + gate * o


def _post(x, yb, mod, g_pre, g_post, wa, wc, wg, wbr, wout, conv_w, conv_b, ln_g, ln_b, w_s, bs_full, tm):
    bsz, s, d = x.shape
    halo = 8

    def const(a):
        return pl.BlockSpec(a.shape, lambda b, i: (0,) * a.ndim, pipeline_mode=pl.Buffered(1))

    consts = (g_pre, g_post, wa, wc, wg, wbr, wout, conv_w, conv_b, ln_g, ln_b, w_s, bs_full)
    return pl.pallas_call(
        functools.partial(_post_kernel, tm=tm, cw=256),
        grid=(bsz, s // tm),
        in_specs=[pl.BlockSpec((1, tm, d), lambda b, i: (b, i, 0)),
                  pl.BlockSpec((1, halo, d), lambda b, i: (b, jnp.maximum(i * (tm // halo) - 1, 0), 0)),
                  pl.BlockSpec((1, tm, MIX_W), lambda b, i: (b, i, 0)),
                  pl.BlockSpec((1, 1, 3 * d), lambda b, i: (b, 0, 0))] + [const(a) for a in consts],
        out_specs=pl.BlockSpec((1, tm, d), lambda b, i: (b, i, 0)),
        out_shape=jax.ShapeDtypeStruct((bsz, s, d), f32),
        compiler_params=pltpu.CompilerParams(vmem_limit_bytes=VMEM_LIMIT),
        name="post",
    )(x, x, yb, mod, *consts)


def _pack_pairs(a, b):
    d = a.shape[0]
    return jnp.concatenate([a.reshape(d, N_KV, HEAD_DIM), b.reshape(d, N_KV, HEAD_DIM)], axis=-1).reshape(d, -1)


def _attn_weight(w_in_b):
    d = w_in_b.shape[0]
    sizes = [MIX_W] + [KV_W] * 6 + [MIX_W, 3 * N_HEADS]
    offs = [0]
    for sz in sizes:
        offs.append(offs[-1] + sz)
    q, kc, vc, ks, vs, kw, vw, z, gl = [w_in_b[:, offs[k]:offs[k + 1]] for k in range(len(sizes))]
    gl = jnp.pad(gl.reshape(d, N_KV, HPG * 3), ((0, 0), (0, 0), (0, LANES - HPG * 3))).reshape(d, N_KV * LANES)
    return jnp.concatenate([q, kc, vc, _pack_pairs(ks, vs), _pack_pairs(kw, vw), z, gl], axis=1).astype(bf16)


def _block_rows(t, s):
    bsz = t.shape[0]
    t = t.reshape(bsz, s // STRIDE_CMP, STRIDE_CMP, N_KV, HEAD_DIM).transpose(0, 3, 1, 2, 4)
    return t.reshape(bsz, N_KV, s // STRIDE_CMP, STRIDE_CMP * HEAD_DIM)


def _layer(x, mod, slopes, g_pre, g_post, w_in, conv_w, conv_b, pos_ck, w_ck1, w_ck2, pos_cv, w_cv1, w_cv2,
           ln_g, ln_b, w_s, b_s, w_br, w_out):
    bsz, s, d = x.shape
    assert s % KEY_CHUNK == 0 and s // L_SEL <= HEAD_DIM and s % 512 == 0
    row = lambda a: a.reshape(1, -1)
    pb = _proj(x, mod, row(g_pre), _attn_weight(w_in[:, B_OFF:B_OFF + B_COLS]), tm=512)

    zpad = jnp.zeros((CMP_HIDDEN, HEAD_DIM), f32)
    kvc = _compress(
        _block_rows(pb[:, :, KC_COL:KC_COL + KV_W], s), _block_rows(pb[:, :, VC_COL:VC_COL + KV_W], s),
        pos_ck.reshape(2, -1), pos_cv.reshape(2, -1), w_ck1.astype(bf16), w_cv1.astype(bf16),
        jnp.concatenate([w_ck2, zpad], axis=1).astype(bf16), jnp.concatenate([zpad, w_cv2], axis=1).astype(bf16))
    yb = _attention(slopes, pb, kvc, tq=Q_TILE)

    bs_full = jnp.repeat(b_s.T, MIX_W // GM_GROUPS, axis=1)
    return _post(x, yb, mod, row(g_pre), row(g_post),
                 w_in[:, A_OFF:A_OFF + A_COLS].astype(bf16), w_in[:, C_OFF:C_OFF + C_COLS].astype(bf16),
                 w_in[:, G_OFF:G_OFF + N_BRANCH * D_MODEL].astype(bf16), w_br.astype(bf16), w_out.astype(bf16),
                 conv_w, row(conv_b), row(ln_g), row(ln_b), w_s, bs_full, tm=512)


def kernel(x, c, g_pre, g_post, w_ada, b_ada, w_in, conv_w, conv_b, pos_ck, w_ck1, w_ck2, pos_cv, w_cv1, w_cv2,
           ln_g, ln_b, w_s, b_s, w_br, w_out):
    depth = w_in.shape[0]
    bsz = x.shape[0]
    mods = _ada(c, w_ada, b_ada).reshape(depth, bsz, 1, 3 * D_MODEL)
    head = jnp.arange(1, N_HEADS + 1, dtype=f32)
    slopes = (2.0 ** (-8.0 * head / N_HEADS)).reshape(N_KV, HPG)
    for l in range(depth):
        x = _layer(x, mods[l], slopes, g_pre[l], g_post[l], w_in[l], conv_w[l], conv_b[l], pos_ck[l], w_ck1[l],
                   w_ck2[l], pos_cv[l], w_cv1[l], w_cv2[l], ln_g[l], ln_b[l], w_s[l], b_s[l], w_br[l], w_out[l])
    return x
```

```python
import functools

import jax
import jax.numpy as jnp
from jax import lax
from jax.experimental import pallas as pl
from jax.experimental.pallas import tpu as pltpu

D_MODEL = 1024
MIX_W = 1024
CONV_K = 3
N_HEADS = 16
HEAD_DIM = 64
N_KV = 4
HPG = N_HEADS // N_KV
KV_W = N_KV * HEAD_DIM
L_CMP = 32
STRIDE_CMP = 16
CMP_HIDDEN = 128
L_SEL = 64
N_SEL_BLOCKS = 16
WINDOW = 512
CHUNK = 128
GM_GROUPS = 8
N_BRANCH = 3
EPS = 1e-6
NEG = -1e30

A_OFF = 0
A_COLS = 4 * MIX_W
B_OFF = A_OFF + A_COLS
B_COLS = 2 * MIX_W + 6 * KV_W + 3 * N_HEADS
C_OFF = B_OFF + B_COLS
C_COLS = 3 * MIX_W
G_OFF = C_OFF + C_COLS

LANES = 128
PAIR_W = 2 * HEAD_DIM
Q_COL = 0
KC_COL = Q_COL + MIX_W
VC_COL = KC_COL + KV_W
KVS_COL = VC_COL + KV_W
KVW_COL = KVS_COL + N_KV * PAIR_W
Z_COL = KVW_COL + N_KV * PAIR_W
GL_COL = Z_COL + MIX_W
PB_COLS = GL_COL + N_KV * LANES

KEY_CHUNK = 512
Q_TILE = 512
MASK_BIG = 2.0 ** 100
LOG2E = 1.4426950408889634
ONES_ROWS = 16
VMEM_LIMIT = 60 * 1024 * 1024

f32 = jnp.float32
bf16 = jnp.bfloat16


def _dot(a, b):
    return jnp.dot(a, b, preferred_element_type=f32)


def _silu(x):
    return x * jax.nn.sigmoid(x)


def _norm_mod(x, g, scl, shift):
    y = x * lax.rsqrt(jnp.mean(x * x, axis=-1, keepdims=True) + EPS)
    return (y * g) * (1.0 + scl) + shift


def _ada_kernel(c_ref, w_ref, b_ref, o_ref):
    c = c_ref[...]
    o_ref[0] = jnp.dot(_silu(c), w_ref[0], preferred_element_type=f32,
                       precision=lax.Precision.HIGHEST) + b_ref[0]


def _ada(c, w_ada, b_ada):
    depth, d, n = w_ada.shape
    bsz = c.shape[0]
    return pl.pallas_call(
        _ada_kernel,
        grid=(depth,),
        in_specs=[pl.BlockSpec((bsz, d), lambda l: (0, 0)),
                  pl.BlockSpec((1, d, n), lambda l: (l, 0, 0)),
                  pl.BlockSpec((1, 1, n), lambda l: (l, 0, 0))],
        out_specs=pl.BlockSpec((1, bsz, n), lambda l: (l, 0, 0)),
        out_shape=jax.ShapeDtypeStruct((depth, bsz, n), f32),
        compiler_params=pltpu.CompilerParams(vmem_limit_bytes=VMEM_LIMIT),
        name="ada",
    )(c, w_ada, b_ada.reshape(depth, 1, n))


def _proj_kernel(x_ref, mod_ref, g_ref, w_ref, o_ref, *, col_step):
    mod = mod_ref[0]
    h = _norm_mod(x_ref[0], g_ref[...], mod[:, D_MODEL:2 * D_MODEL], mod[:, :D_MODEL]).astype(bf16)
    for c0 in range(0, PB_COLS, col_step):
        o_ref[0, :, c0:c0 + col_step] = _dot(h, w_ref[:, c0:c0 + col_step]).astype(bf16)


def _proj(x, mod, g_pre, wb, tm):
    bsz, s, d = x.shape
    return pl.pallas_call(
        functools.partial(_proj_kernel, col_step=512),
        grid=(bsz, s // tm),
        in_specs=[pl.BlockSpec((1, tm, d), lambda b, i: (b, i, 0)),
                  pl.BlockSpec((1, 1, 3 * d), lambda b, i: (b, 0, 0)),
                  pl.BlockSpec((1, d), lambda b, i: (0, 0)),
                  pl.BlockSpec((d, PB_COLS), lambda b, i: (0, 0))],
        out_specs=pl.BlockSpec((1, tm, PB_COLS), lambda b, i: (b, i, 0)),
        out_shape=jax.ShapeDtypeStruct((bsz, s, PB_COLS), bf16),
        compiler_params=pltpu.CompilerParams(vmem_limit_bytes=VMEM_LIMIT),
        name="proj",
    )(x, mod, g_pre, wb)


def _cmp_kernel(tk_ref, tv_ref, pk_ref, pv_ref, w1k_ref, w1v_ref, w2k_ref, w2v_ref, o_ref):
    half = STRIDE_CMP * HEAD_DIM

    def hidden(t_ref, p_ref, w1_ref):
        t = t_ref[0, 0].astype(f32)
        n_rows = t.shape[0]
        ta = (t + p_ref[0:1, :]).astype(bf16)
        tb = (t + p_ref[1:2, :]).astype(bf16)
        a = _dot(ta, w1_ref[0:half, :])
        b = _dot(tb, w1_ref[half:2 * half, :])
        return _silu(a + pltpu.roll(b, n_rows - 1, axis=0)).astype(bf16)

    hk = hidden(tk_ref, pk_ref, w1k_ref)
    hv = hidden(tv_ref, pv_ref, w1v_ref)
    o_ref[0, 0] = (_dot(hk, w2k_ref[...]) + _dot(hv, w2v_ref[...])).astype(bf16)


def _compress(tk, tv, posk, posv, w1k, w1v, w2k, w2v):
    bsz, g, n_rows, width = tk.shape
    tspec = pl.BlockSpec((1, 1, n_rows, width), lambda b, j: (b, j, 0, 0))

    def full(a):
        return pl.BlockSpec(a.shape, lambda b, j: (0,) * a.ndim)

    return pl.pallas_call(
        _cmp_kernel,
        grid=(bsz, g),
        in_specs=[tspec, tspec, full(posk), full(posv), full(w1k), full(w1v), full(w2k), full(w2v)],
        out_specs=pl.BlockSpec((1, 1, n_rows, PAIR_W), lambda b, j: (b, j, 0, 0)),
        out_shape=jax.ShapeDtypeStruct((bsz, g, n_rows, PAIR_W), bf16),
        compiler_params=pltpu.CompilerParams(vmem_limit_bytes=VMEM_LIMIT),
        name="compress",
    )(tk, tv, posk, posv, w1k, w1v, w2k, w2v)


def _attn_kernel(slopes_ref, q_ref, kvs_ref, kvst_ref, kvw_ref, kvwt_ref, kvc_ref, kvct_ref, kfeat_ref, cfeat_ref,
                 z_ref, gl_ref, o_ref, *, tq, n_cmp_pad):
    g = pl.program_id(1)
    qt = pl.program_id(2)
    t0 = qt * tq
    cols = HPG * tq
    kc = KEY_CHUNK
    c_hi = t0 // kc

    qf = q_ref[0].astype(f32) * (HEAD_DIM ** -0.5 * LOG2E)
    qts = []
    for m in range(HPG // 2):
        t = qf[:, m * LANES:(m + 1) * LANES].T
        qts += [t[:HEAD_DIM], t[HEAD_DIM:]]
    q_t = jnp.concatenate(qts, axis=1)

    lane = lax.broadcasted_iota(jnp.int32, (1, cols), 1)
    head = lane // tq
    trow = lane - head * tq
    slope = LOG2E * jnp.where(head == 0, slopes_ref[g, 0],
                              jnp.where(head == 1, slopes_ref[g, 1],
                                        jnp.where(head == 2, slopes_ref[g, 2], slopes_ref[g, 3])))
    s_hi = slope.astype(bf16).astype(f32)
    s_mid = (slope - s_hi).astype(bf16).astype(f32)
    s_lo = slope - s_hi - s_mid
    r8 = lax.broadcasted_iota(jnp.int32, (8, cols), 0)
    blk = float(L_SEL)
    feat = jnp.where(r8 == 0, blk * s_hi, jnp.where(r8 == 1, blk * s_mid, jnp.where(r8 == 2, blk * s_lo,
           jnp.where(r8 == 3, s_hi, jnp.where(r8 == 4, s_mid, jnp.where(r8 == 5, s_lo, 0.0))))))
    zeros_h = jnp.zeros((HEAD_DIM, cols), f32)
    top = jnp.concatenate([q_t, zeros_h, feat, jnp.zeros((HEAD_DIM - 8, cols), f32)], axis=0)
    qw_t = jnp.concatenate([top, zeros_h], axis=0).astype(bf16)

    kvc = kvc_ref[0, 0]
    s = _dot(jnp.concatenate([kvc, cfeat_ref[...]], axis=1), qw_t)
    last_c = (t0 + trow - (L_CMP - 1)) // STRIDE_CMP
    s = jnp.where(lax.broadcasted_iota(jnp.int32, (n_cmp_pad, cols), 0) <= last_c, s, NEG)
    e = jnp.exp2(s - jnp.max(s, axis=0, keepdims=True))
    l_c = jnp.sum(e, axis=0, keepdims=True)
    p_c = e * jnp.where(last_c >= 0, 1.0 / l_c, 0.0)
    o_cmp = _dot(kvct_ref[0, 0], p_c.astype(bf16))

    assert kc <= WINDOW
    key_r = lax.broadcasted_iota(jnp.int32, (kc, cols), 0)

    def causal(c, s_chunk):
        return jnp.where(key_r <= trow + (t0 - c * kc), s_chunk, NEG)

    def load(c, ref, reft):
        k0 = pl.multiple_of(c * kc, kc)
        k_aug = jnp.concatenate([ref[0, pl.ds(k0, kc), :], kfeat_ref[pl.ds(k0, kc), :]], axis=1)
        return k_aug, jnp.concatenate([reft[0, :, pl.ds(k0, kc)], jnp.ones((ONES_ROWS, kc), bf16)], axis=0)

    def softmax_step(carry, s_chunk, vt1):
        m_prev, acc = carry
        m_new = jnp.maximum(m_prev, jnp.max(s_chunk, axis=0, keepdims=True))
        p = jnp.exp2(s_chunk - m_new)
        return m_new, jnp.exp2(m_prev - m_new) * acc + _dot(vt1, p.astype(bf16))

    def first_step(s_chunk, vt1):
        m = jnp.max(s_chunk, axis=0, keepdims=True)
        return m, _dot(vt1, jnp.exp2(s_chunk - m).astype(bf16))

    def finish(carry):
        acc = carry[1]
        return acc[:HEAD_DIM] * (1.0 / acc[HEAD_DIM:HEAD_DIM + 1])

    k_aug, kvt = load(c_hi, kvw_ref, kvwt_ref)
    win_carry = first_step(causal(c_hi, _dot(k_aug, qw_t)), kvt)

    p_sum = p_c[:, 0:tq] + p_c[:, tq:2 * tq] + p_c[:, 2 * tq:3 * tq] + p_c[:, 3 * tq:4 * tq]
    p_hi = p_sum.astype(bf16)
    r1 = p_sum - p_hi.astype(f32)
    p_mid = r1.astype(bf16)
    p_lo = (r1 - p_mid.astype(f32)).astype(bf16)
    n_blk = HEAD_DIM
    blk_r = lax.broadcasted_iota(jnp.int32, (n_blk, n_cmp_pad), 0)
    cmp_c = lax.broadcasted_iota(jnp.int32, (n_blk, n_cmp_pad), 1)
    ratio = L_SEL // STRIDE_CMP
    ovl = jnp.where((cmp_c >= ratio * blk_r - (L_CMP // STRIDE_CMP - 1)) & (cmp_c <= ratio * blk_r + ratio - 1),
                    1.0, 0.0).astype(bf16)
    imp = jnp.maximum(_dot(ovl, p_hi) + _dot(ovl, p_mid) + _dot(ovl, p_lo), 0.0)
    srow = lax.broadcasted_iota(jnp.int32, (n_blk, tq), 0)
    cur = (t0 + lax.broadcasted_iota(jnp.int32, (n_blk, tq), 1)) // L_SEL
    forced = (srow == 0) | (srow == cur) | (srow == cur - 1)
    work = jnp.where((srow >= 1) & (srow <= cur - 2), imp, -1.0)
    sel = jnp.where(forced, 1.0, 0.0)
    srow_f = srow.astype(f32)
    for _ in range(N_SEL_BLOCKS - 3):
        best = jnp.max(work, axis=0, keepdims=True)
        first = jnp.min(jnp.where(work == best, srow_f, float(n_blk)), axis=0, keepdims=True)
        pick = (srow_f == first) & (best >= 0.0)
        sel = jnp.where(pick, 1.0, sel)
        work = jnp.where(pick, -1.0, work)
    pen = (sel - 1.0) * MASK_BIG
    qs_t = jnp.concatenate([top, jnp.concatenate([pen] * HPG, axis=1)], axis=0).astype(bf16)

    for back in range(1, (WINDOW + kc - 1) // kc + 1):
        c = c_hi - back
        k_aug, kvt = load(jnp.maximum(c, 0), kvw_ref, kvwt_ref)
        low = jnp.where(c >= 0, trow + (t0 - c * kc - WINDOW), kc)
        win_carry = softmax_step(win_carry, jnp.where(key_r > low, _dot(k_aug, qw_t), NEG), kvt)
    o_win = finish(win_carry)

    k_aug, kvt = load(c_hi, kvs_ref, kvst_ref)
    carry = first_step(causal(c_hi, _dot(k_aug, qs_t)), kvt)

    def slc_step(c, carry):
        k_aug, kvt = load(c, kvs_ref, kvst_ref)
        return softmax_step(carry, _dot(k_aug, qs_t), kvt)

    def slc_pair(i, streams):
        return slc_step(2 * i, streams[0]), slc_step(2 * i + 1, streams[1])

    empty = (jnp.full((1, cols), NEG, f32), jnp.zeros((HEAD_DIM + ONES_ROWS, cols), f32))
    st_a, st_b = lax.fori_loop(0, c_hi // 2, slc_pair, (carry, empty))
    st_a = lax.fori_loop(0, c_hi % 2, lambda _, st: slc_step(c_hi - 1, st), st_a)
    m_ab = jnp.maximum(st_a[0], st_b[0])
    o_slc = finish((m_ab, st_a[1] * jnp.exp2(st_a[0] - m_ab) + st_b[1] * jnp.exp2(st_b[0] - m_ab)))

    gates_t = jax.nn.sigmoid(gl_ref[0].astype(f32)).T
    outs = []
    for n in range(HPG):
        cs = slice(n * tq, (n + 1) * tq)
        outs.append(gates_t[3 * n:3 * n + 1] * o_cmp[:, cs] + gates_t[3 * n + 1:3 * n + 2] * o_slc[:, cs]
                    + gates_t[3 * n + 2:3 * n + 3] * o_win[:, cs])
    slabs = [jnp.concatenate(outs[2 * m:2 * m + 2], axis=0).T for m in range(HPG // 2)]
    o = jnp.concatenate(slabs, axis=1)
    o_ref[0] = (o * _silu(z_ref[0].astype(f32))).astype(bf16)


def _key_features(pos):
    hi, lo = (pos // L_SEL).astype(f32), (pos % L_SEL).astype(f32)
    lanes = jnp.arange(LANES)
    feat = jnp.where(lanes[None, :] < 3, hi[:, None], jnp.where(lanes[None, :] < 6, lo[:, None], 0.0))
    onehot = (lanes[None, :] - HEAD_DIM == (pos // L_SEL)[:, None]).astype(f32)
    return (feat + onehot).astype(bf16)


def _attention(slopes, pb, kvc, tq):
    bsz, s, _ = pb.shape
    n_cmp_pad = kvc.shape[2]
    grp_w = HPG * HEAD_DIM
    pair_cols = N_KV * PAIR_W
    def values_t(col):
        pairs = pb[:, :, col:col + pair_cols].reshape(bsz, s, N_KV, 2, HEAD_DIM)
        return pairs[:, :, :, 1].reshape(bsz, s, KV_W).transpose(0, 2, 1)

    kvst, kvwt = values_t(KVS_COL), values_t(KVW_COL)
    kvct = kvc[:, :, :, HEAD_DIM:].transpose(0, 1, 3, 2)
    kfeat = _key_features(jnp.arange(s))
    cpos = jnp.arange(n_cmp_pad) * STRIDE_CMP + (L_CMP - 1)
    cfeat = jnp.where(jnp.arange(LANES)[None, :] < HEAD_DIM, _key_features(cpos), 0).astype(bf16)
    return pl.pallas_call(
        functools.partial(_attn_kernel, tq=tq, n_cmp_pad=n_cmp_pad),
        grid=(bsz, N_KV, s // tq),
        in_specs=[pl.BlockSpec(memory_space=pltpu.SMEM),
                  pl.BlockSpec((1, tq, grp_w), lambda b, g, i: (b, i, Q_COL // grp_w + g)),
                  pl.BlockSpec((1, s, PAIR_W), lambda b, g, i: (b, 0, KVS_COL // PAIR_W + g)),
                  pl.BlockSpec((1, HEAD_DIM, s), lambda b, g, i: (b, g, 0)),
                  pl.BlockSpec((1, s, PAIR_W), lambda b, g, i: (b, 0, KVW_COL // PAIR_W + g)),
                  pl.BlockSpec((1, HEAD_DIM, s), lambda b, g, i: (b, g, 0)),
                  pl.BlockSpec((1, 1, n_cmp_pad, PAIR_W), lambda b, g, i: (b, g, 0, 0)),
                  pl.BlockSpec((1, 1, HEAD_DIM, n_cmp_pad), lambda b, g, i: (b, g, 0, 0)),
                  pl.BlockSpec((s, LANES), lambda b, g, i: (0, 0)),
                  pl.BlockSpec((n_cmp_pad, LANES), lambda b, g, i: (0, 0)),
                  pl.BlockSpec((1, tq, grp_w), lambda b, g, i: (b, i, Z_COL // grp_w + g)),
                  pl.BlockSpec((1, tq, LANES), lambda b, g, i: (b, i, GL_COL // LANES + g))],
        out_specs=pl.BlockSpec((1, tq, grp_w), lambda b, g, i: (b, i, g)),
        out_shape=jax.ShapeDtypeStruct((bsz, s, MIX_W), bf16),
        compiler_params=pltpu.CompilerParams(vmem_limit_bytes=VMEM_LIMIT),
        name="attn",
    )(slopes, pb, pb, kvst, pb, kvwt, kvc, kvct, kfeat, cfeat, pb, pb)


def _post_kernel(x_ref, xh_ref, yb_ref, mod_ref, gpre_ref, gpost_ref, wa_ref, wc_ref, wg_ref, wbr_ref, wout_ref,
                 convw_ref, convb_ref, lng_ref, lnb_ref, ws_ref, bs_ref, o_ref, *, tm, cw):
    i = pl.program_id(1)
    mod = mod_ref[0]
    shift, scl, gate = mod[:, :D_MODEL], mod[:, D_MODEL:2 * D_MODEL], mod[:, 2 * D_MODEL:]
    x = x_ref[0]
    h32 = _norm_mod(x, gpre_ref[...], scl, shift)
    h = h32.astype(bf16)
    halo = xh_ref.shape[1]
    h_ext = jnp.concatenate([_norm_mod(xh_ref[0], gpre_ref[...], scl, shift), h32], axis=0).astype(bf16)
    row = lax.broadcasted_iota(jnp.int32, (tm, cw), 0)
    n_cc = MIX_W // cw

    acc_a = jnp.zeros((tm, D_MODEL), f32)
    for j in range(n_cc):
        def col(k):
            return slice(k * MIX_W + j * cw, k * MIX_W + (j + 1) * cw)
        b_ = _dot(h, wa_ref[:, col(0)])
        y_ext = _dot(h_ext, wa_ref[:, col(1)]) * _dot(h_ext, wa_ref[:, col(2)])
        y = y_ext[halo:]
        z = _dot(h, wa_ref[:, col(3)])
        yh = jnp.where(i > 0, y_ext[:halo], 0.0)
        y1 = jnp.where(row == 0, yh[7:8, :], pltpu.roll(y, 1, axis=0))
        y2 = jnp.where(row == 0, yh[6:7, :], jnp.where(row == 1, yh[7:8, :], pltpu.roll(y, 2, axis=0)))
        cs = slice(j * cw, (j + 1) * cw)
        conv = (convb_ref[:, cs] + convw_ref[0:1, cs] * y2 + convw_ref[1:2, cs] * y1
                + convw_ref[2:3, cs] * y)
        ya = (b_ * conv * _silu(z)).astype(bf16)
        acc_a = acc_a + _dot(ya, wbr_ref[0, cs, :])
    merged = jax.nn.sigmoid(_dot(h, wg_ref[:, 0:D_MODEL])) * acc_a

    merged = merged + jax.nn.sigmoid(_dot(h, wg_ref[:, D_MODEL:2 * D_MODEL])) * _dot(yb_ref[0], wbr_ref[1])

    v = jax.nn.gelu(_dot(h, wc_ref[:, MIX_W:2 * MIX_W]))
    mu = jnp.mean(v, axis=-1, keepdims=True)
    vc = v - mu
    vn = (vc * lax.rsqrt(jnp.mean(vc * vc, axis=-1, keepdims=True) + EPS)) * lng_ref[...] + lnb_ref[...]
    vn = vn.astype(bf16)
    ri = lax.broadcasted_iota(jnp.int32, (CHUNK, CHUNK), 0)
    ci = lax.broadcasted_iota(jnp.int32, (CHUNK, CHUNK), 1)
    acc_c = jnp.zeros((tm, D_MODEL), f32)
    gw = MIX_W // GM_GROUPS
    n_ck = tm // CHUNK
    for gp in range(GM_GROUPS // 2):
        cs2 = slice(2 * gp * gw, (2 * gp + 2) * gw)
        sps = []
        for gi in (2 * gp, 2 * gp + 1):
            cs = slice(gi * gw, (gi + 1) * gw)
            wm = jnp.where(ri >= ci, ws_ref[gi], 0.0).astype(bf16)
            r = _dot(wm, jnp.concatenate([vn[k * CHUNK:(k + 1) * CHUNK, cs] for k in range(n_ck)], axis=1))
            sps.append(jnp.concatenate([r[:, k * gw:(k + 1) * gw] for k in range(n_ck)], axis=0))
        sp = jnp.concatenate(sps, axis=1) + jnp.concatenate([bs_ref[:, cs2]] * n_ck, axis=0)
        u = jax.nn.gelu(_dot(h, wc_ref[:, cs2]))
        z = _dot(h, wc_ref[:, 2 * MIX_W + 2 * gp * gw:2 * MIX_W + (2 * gp + 2) * gw])
        yc = (u * sp * _silu(z)).astype(bf16)
        acc_c = acc_c + _dot(yc, wbr_ref[2, cs2, :])
    merged = merged + jax.nn.sigmoid(_dot(h, wg_ref[:, 2 * D_MODEL:3 * D_MODEL])) * acc_c

    o = _dot(merged.astype(bf16), wout_ref[...])
    o = o * lax.rsqrt(jnp.mean(o * o, axis=-1, keepdims=True) + EPS) * gpost_ref[...]
    o_ref[0] = x + gate * o


def _post(x, yb, mod, g_pre, g_post, wa, wc, wg, wbr, wout, conv_w, conv_b, ln_g, ln_b, w_s, bs_full, tm):
    bsz, s, d = x.shape
    halo = 8

    def const(a):
        return pl.BlockSpec(a.shape, lambda b, i: (0,) * a.ndim, pipeline_mode=pl.Buffered(1))

    consts = (g_pre, g_post, wa, wc, wg, wbr, wout, conv_w, conv_b, ln_g, ln_b, w_s, bs_full)
    return pl.pallas_call(
        functools.partial(_post_kernel, tm=tm, cw=256),
        grid=(bsz, s // tm),
        in_specs=[pl.BlockSpec((1, tm, d), lambda b, i: (b, i, 0)),
                  pl.BlockSpec((1, halo, d), lambda b, i: (b, jnp.maximum(i * (tm // halo) - 1, 0), 0)),
                  pl.BlockSpec((1, tm, MIX_W), lambda b, i: (b, i, 0)),
                  pl.BlockSpec((1, 1, 3 * d), lambda b, i: (b, 0, 0))] + [const(a) for a in consts],
        out_specs=pl.BlockSpec((1, tm, d), lambda b, i: (b, i, 0)),
        out_shape=jax.ShapeDtypeStruct((bsz, s, d), f32),
        compiler_params=pltpu.CompilerParams(vmem_limit_bytes=VMEM_LIMIT),
        name="post",
    )(x, x, yb, mod, *consts)


def _pack_pairs(a, b):
    d = a.shape[0]
    return jnp.concatenate([a.reshape(d, N_KV, HEAD_DIM), b.reshape(d, N_KV, HEAD_DIM)], axis=-1).reshape(d, -1)


def _attn_weight(w_in_b):
    d = w_in_b.shape[0]
    sizes = [MIX_W] + [KV_W] * 6 + [MIX_W, 3 * N_HEADS]
    offs = [0]
    for sz in sizes:
        offs.append(offs[-1] + sz)
    q, kc, vc, ks, vs, kw, vw, z, gl = [w_in_b[:, offs[k]:offs[k + 1]] for k in range(len(sizes))]
    gl = jnp.pad(gl.reshape(d, N_KV, HPG * 3), ((0, 0), (0, 0), (0, LANES - HPG * 3))).reshape(d, N_KV * LANES)
    return jnp.concatenate([q, kc, vc, _pack_pairs(ks, vs), _pack_pairs(kw, vw), z, gl], axis=1).astype(bf16)


def _block_rows(t, s):
    bsz = t.shape[0]
    t = t.reshape(bsz, s // STRIDE_CMP, STRIDE_CMP, N_KV, HEAD_DIM).transpose(0, 3, 1, 2, 4)
    return t.reshape(bsz, N_KV, s // STRIDE_CMP, STRIDE_CMP * HEAD_DIM)


def _layer(x, mod, slopes, g_pre, g_post, w_in, conv_w, conv_b, pos_ck, w_ck1, w_ck2, pos_cv, w_cv1, w_cv2,
           ln_g, ln_b, w_s, b_s, w_br, w_out):
    bsz, s, d = x.shape
    assert s % KEY_CHUNK == 0 and s // L_SEL <= HEAD_DIM and s % 512 == 0
    row = lambda a: a.reshape(1, -1)
    pb = _proj(x, mod, row(g_pre), _attn_weight(w_in[:, B_OFF:B_OFF + B_COLS]), tm=512)

    zpad = jnp.zeros((CMP_HIDDEN, HEAD_DIM), f32)
    kvc = _compress(
        _block_rows(pb[:, :, KC_COL:KC_COL + KV_W], s), _block_rows(pb[:, :, VC_COL:VC_COL + KV_W], s),
        pos_ck.reshape(2, -1), pos_cv.reshape(2, -1), w_ck1.astype(bf16), w_cv1.astype(bf16),
        jnp.concatenate([w_ck2, zpad], axis=1).astype(bf16), jnp.concatenate([zpad, w_cv2], axis=1).astype(bf16))
    yb = _attention(slopes, pb, kvc, tq=Q_TILE)

    bs_full = jnp.repeat(b_s.T, MIX_W // GM_GROUPS, axis=1)
    return _post(x, yb, mod, row(g_pre), row(g_post),
                 w_in[:, A_OFF:A_OFF + A_COLS].astype(bf16), w_in[:, C_OFF:C_OFF + C_COLS].astype(bf16),
                 w_in[:, G_OFF:G_OFF + N_BRANCH * D_MODEL].astype(bf16), w_br.astype(bf16), w_out.astype(bf16),
                 conv_w, row(conv_b), row(ln_g), row(ln_b), w_s, bs_full, tm=512)


def kernel(x, c, g_pre, g_post, w_ada, b_ada, w_in, conv_w, conv_b, pos_ck, w_ck1, w_ck2, pos_cv, w_cv1, w_cv2,
           ln_g, ln_b, w_s, b_s, w_br, w_out):
    depth = w_in.shape[0]
    bsz = x.shape[0]
    mods = _ada(c, w_ada, b_ada).reshape(depth, bsz, 1, 3 * D_MODEL)
    head = jnp.arange(1, N_HEADS + 1, dtype=f32)
    slopes = (2.0 ** (-8.0 * head / N_HEADS)).reshape(N_KV, HPG)
    for l in range(depth):
        x = _layer(x, mods[l], slopes, g_pre[l], g_post[l], w_in[l], conv_w[l], conv_b[l], pos_ck[l], w_ck1[l],
                   w_ck2[l], pos_cv[l], w_cv1[l], w_cv2[l], ln_g[l], ln_b[l], w_s[l], b_s[l], w_br[l], w_out[l])
    return x
```
